```python
import jax, jax.numpy as jnp
from jax import lax
import numpy as np

D_MODEL = 2048
BATCH = 2
SEQ = 4096
DEPTH = 4

GRID_W = 64
CTX_LEN = 256
MIX_WIDTH = D_MODEL
N_DIR = 2
LRU_WIDTH = MIX_WIDTH // 2
LRU_HEADS = 16
LRU_HEAD_DIM = LRU_WIDTH // LRU_HEADS
CONV_WIDTH = 4
LRU_C = 8.0
RWKV_WIDTH = MIX_WIDTH - LRU_WIDTH
RWKV_HEAD_DIM = 64
RWKV_HEADS = RWKV_WIDTH // RWKV_HEAD_DIM
DECAY_LORA = 64
AAA_LORA = 64
GATE_LORA = 160
RWKV_COLS = 3 * RWKV_WIDTH + N_DIR * DECAY_LORA + N_DIR * AAA_LORA + GATE_LORA
IN_COLS = 2 * LRU_WIDTH + RWKV_COLS
PEER_HEADS = 8
PEER_N_KEYS = 128
PEER_N_EXPERTS = PEER_N_KEYS * PEER_N_KEYS
PEER_D_QUERY = 256
PEER_TOPK = 16
PEER_BLOCK = 128
ALPHA = (2.0 * DEPTH) ** 0.25
BETA = (8.0 * DEPTH) ** -0.25
LN_EPS = 1e-5
GN_EPS = 64e-5

kernel_name = 'hybrid_rglru_rwkv7_peer_dit_trunk'


def layer_norm(z, g, b):
    zf = z.astype(jnp.float32)
    mu = jnp.mean(zf, -1, keepdims=True)
    var = jnp.mean(jnp.square(zf - mu), -1, keepdims=True)
    return ((zf - mu) * lax.rsqrt(var + LN_EPS) * g + b).astype(z.dtype)


def modulate(z, shift, scale):
    return z * (1.0 + scale) + shift


def grid_pos_embed(rows, dim):
    t = jnp.arange(rows * GRID_W)
    row = (t // GRID_W).astype(jnp.float32)
    col = (t % GRID_W).astype(jnp.float32)
    quarter = dim // 4
    freq = 1.0 / (10000.0 ** (jnp.arange(quarter, dtype=jnp.float32) / quarter))

    def sincos(p):
        ang = p[:, None] * freq[None, :]
        return jnp.concatenate([jnp.sin(ang), jnp.cos(ang)], axis=-1)

    return jnp.concatenate([sincos(row), sincos(col)], axis=-1)


def shift_prev(z):
    return jnp.pad(z, ((0, 0), (1, 0), (0, 0)))[:, :-1]


def shift_next(z):
    return jnp.pad(z, ((0, 0), (0, 1), (0, 0)))[:, 1:]


def centred_dwconv(z, w, b):
    T = z.shape[1]
    left = CONV_WIDTH // 2
    zp = jnp.pad(z, ((0, 0), (left, CONV_WIDTH - 1 - left), (0, 0)))
    out = b
    for tap in range(CONV_WIDTH):
        out = out + zp[:, tap:tap + T] * w[tap]
    return out


def linear_scan(a, b, h0, reverse):
    if reverse:
        a, b = a[:, ::-1], b[:, ::-1]

    def combine(left, right):
        a_l, b_l = left
        a_r, b_r = right
        return a_l * a_r, a_r * b_l + b_r

    a_cum, b_cum = lax.associative_scan(combine, (a, b), axis=1)
    h = b_cum + a_cum * h0[:, None, :]
    return h[:, ::-1] if reverse else h


def block_diag_linear(z, w, b):
    zh = z.reshape(z.shape[:-1] + (LRU_HEADS, LRU_HEAD_DIM))
    out = jnp.einsum('bthi,hij->bthj', zh, w)
    return out.reshape(z.shape) + b


def rglru_coeffs(xc, wa, ba, wx, bx, lam):
    r = jax.nn.sigmoid(block_diag_linear(xc, wa, ba))
    i = jax.nn.sigmoid(block_diag_linear(xc, wx, bx))
    log_a = -LRU_C * r * jax.nn.softplus(-lam)
    return jnp.exp(log_a), jnp.sqrt(-jnp.expm1(2.0 * log_a)) * (i * xc)


def rglru_group(u_ctx, u_lat, conv_w, conv_b, wa, ba, wx, bx, lam):
    xcs = [centred_dwconv(u, conv_w, conv_b).astype(jnp.float32) for u in (u_ctx, u_lat)]
    ys = [jnp.zeros_like(xc) for xc in xcs]
    for d in range(N_DIR):
        reverse = d == 1
        h_end = jnp.zeros_like(xcs[0][:, 0])
        for s, xc in enumerate(xcs):
            a, b_in = rglru_coeffs(xc, wa[d], ba[d], wx[d], bx[d], lam[d])
            h = linear_scan(a, b_in, h_end, reverse)
            h_end = h[:, 0] if reverse else h[:, -1]
            ys[s] = ys[s] + h
    return ys


def to_heads(z):
    return z.reshape(z.shape[:-1] + (RWKV_HEADS, RWKV_HEAD_DIM))


def rwkv7_prepare(rw, mu):
    rw = rw.astype(jnp.float32)
    rw = rw + mu[0] * (shift_prev(rw) - rw) + mu[1] * (shift_next(rw) - rw)
    o = 3 * RWKV_WIDTH
    r, k, v = rw[..., :RWKV_WIDTH], rw[..., RWKV_WIDTH:2 * RWKV_WIDTH], rw[..., 2 * RWKV_WIDTH:o]
    wd = rw[..., o:o + N_DIR * DECAY_LORA].reshape(rw.shape[:-1] + (N_DIR, DECAY_LORA))
    o += N_DIR * DECAY_LORA
    ad = rw[..., o:o + N_DIR * AAA_LORA].reshape(rw.shape[:-1] + (N_DIR, AAA_LORA))
    o += N_DIR * AAA_LORA
    gd = rw[..., o:]
    return r, k, v, wd, ad, gd


def rwkv7_coeffs(k, wd, ad, w0, w2, a0, a2, k_k, k_a):
    w = -jax.nn.softplus(-(w0 + jnp.tanh(wd) @ w2)) - 0.5
    decay = jnp.exp(-jnp.exp(w))
    a = jax.nn.sigmoid(a0 + ad @ a2)
    kk = to_heads(k * k_k)
    kk = kk * lax.rsqrt(jnp.maximum(jnp.sum(kk * kk, -1, keepdims=True), 1e-24))
    kd = k * (1.0 + (a - 1.0) * k_a)
    return to_heads(decay), to_heads(kd), kk, to_heads(a)


def rwkv7_scan(r, w, k, v, a_vec, b_vec, state0, reverse):
    def step(S, inp):
        r_t, w_t, k_t, v_t, a_t, b_t = inp
        sa = jnp.einsum('bhij,bhj->bhi', S, a_t)
        S = S * w_t[:, :, None, :] + sa[..., None] * b_t[:, :, None, :] + v_t[..., None] * k_t[:, :, None, :]
        return S, jnp.einsum('bhij,bhj->bhi', S, r_t)

    xs = tuple(jnp.moveaxis(t, 1, 0) for t in (r, w, k, v, a_vec, b_vec))
    state, ys = lax.scan(step, state0, xs, reverse=reverse)
    return state, jnp.moveaxis(ys, 0, 1)


def head_groupnorm(y, g, b):
    mu = jnp.mean(y, -1, keepdims=True)
    var = jnp.mean(jnp.square(y - mu), -1, keepdims=True)
    yn = (y - mu) * lax.rsqrt(var + GN_EPS)
    return yn.reshape(y.shape[:-2] + (RWKV_WIDTH,)) * g + b


def rwkv7_group(rw_ctx, rw_lat, mu, w0, w2, a0, a2, g2, k_k, k_a, r_k, gn_g, gn_b):
    segs = [rwkv7_prepare(rw, mu) for rw in (rw_ctx, rw_lat)]
    B_ = rw_ctx.shape[0]
    ys = [jnp.zeros(s[0].shape[:-1] + (RWKV_HEADS, RWKV_HEAD_DIM), jnp.float32) for s in segs]
    bonus = [jnp.zeros_like(y) for y in ys]
    for d in range(N_DIR):
        reverse = d == 1
        state = jnp.zeros((B_, RWKV_HEADS, RWKV_HEAD_DIM, RWKV_HEAD_DIM), jnp.float32)
        for s, (r, k, v, wd, ad, gd) in enumerate(segs):
            decay, kd, kk, a = rwkv7_coeffs(k, wd[..., d, :], ad[..., d, :], w0[d], w2[d], a0[d], a2[d], k_k, k_a)
            rh, vh = to_heads(r), to_heads(v)
            state, y = rwkv7_scan(rh, decay, kd, vh, -kk, kk * a, state, reverse)
            ys[s] = ys[s] + y
            bonus[s] = bonus[s] + jnp.sum(rh * kd * r_k, -1, keepdims=True) * vh
    outs = []
    for s, (r, k, v, wd, ad, gd) in enumerate(segs):
        g = jax.nn.sigmoid(gd) @ g2
        o = head_groupnorm(ys[s], gn_g, gn_b) + bonus[s].reshape(r.shape)
        outs.append(o * g)
    return outs


def token_mixer(h_ctx, h_lat, keep_ctx, w_in, conv_w, conv_b, wa, ba, wx, bx, lam,
                mu, w0, w2, a0, a2, g2, k_k, k_a, r_k, gn_g, gn_b, w_out):
    p_ctx, p_lat = h_ctx @ w_in, h_lat @ w_in
    y_lru = rglru_group(p_ctx[..., :LRU_WIDTH], p_lat[..., :LRU_WIDTH],
                        conv_w, conv_b, wa, ba, wx, bx, lam)
    y_rwkv = rwkv7_group(p_ctx[..., 2 * LRU_WIDTH:], p_lat[..., 2 * LRU_WIDTH:],
                         mu, w0, w2, a0, a2, g2, k_k, k_a, r_k, gn_g, gn_b)

    def merge(p, yl, yr):
        gate = jax.nn.gelu(p[..., LRU_WIDTH:2 * LRU_WIDTH].astype(jnp.float32))
        return jnp.concatenate([gate * yl, yr], axis=-1).astype(p.dtype) @ w_out

    o_lat = merge(p_lat, y_lru[1], y_rwkv[1])
    o_ctx = merge(p_ctx, y_lru[0], y_rwkv[0]) if keep_ctx else None
    return o_ctx, o_lat


def peer(h, w_q, sub_keys, u, v):
    B_, T, D = h.shape
    q = (h @ w_q).reshape(B_, T, PEER_HEADS, 2, PEER_D_QUERY // 2)
    s = jnp.einsum('bthpd,hpkd->bthpk', q, sub_keys).astype(jnp.float32)
    s1, i1 = lax.top_k(s[..., 0, :], PEER_TOPK)
    s2, i2 = lax.top_k(s[..., 1, :], PEER_TOPK)
    n_cand = PEER_TOPK * PEER_TOPK
    cand_s = (s1[..., :, None] + s2[..., None, :]).reshape(B_, T, PEER_HEADS, n_cand)
    cand_i = (i1[..., :, None] * PEER_N_KEYS + i2[..., None, :]).reshape(B_, T, PEER_HEADS, n_cand)
    top_s, pos = lax.top_k(cand_s, PEER_TOPK)
    idx = jnp.take_along_axis(cand_i, pos, axis=-1)
    gates = jax.nn.softmax(top_s, axis=-1).astype(h.dtype)
    n_blocks = (B_ * T) // PEER_BLOCK
    hb = h.reshape(n_blocks, PEER_BLOCK, D)
    ib = idx.reshape(n_blocks, PEER_BLOCK, PEER_HEADS * PEER_TOPK)
    gb = gates.reshape(n_blocks, PEER_BLOCK, PEER_HEADS * PEER_TOPK)

    def expert_block(args):
        h_blk, i_blk, g_blk = args
        act = jax.nn.gelu(jnp.einsum('nkd,nd->nk', u[i_blk], h_blk), approximate=False)
        return jnp.einsum('nk,nkd->nd', g_blk * act, v[i_blk])

    return lax.map(expert_block, (hb, ib, gb)).reshape(B_, T, D)


def setup_inputs(seed: int = 0) -> dict:
    key = jax.random.key(seed)
    ks = jax.random.split(key, 40)
    ctr = [0]

    def nxt():
        k = ks[ctr[0]]
        ctr[0] += 1
        return k

    def nrm(shape, std):
        return jax.random.normal(nxt(), shape, jnp.float32) * std

    def uni(shape, lo, hi):
        return jax.random.uniform(nxt(), shape, jnp.float32, lo, hi)

    D = D_MODEL
    x = nrm((BATCH, SEQ, D), 1.0)
    c = nrm((BATCH, D), 1.0)
    ctx = nrm((BATCH, CTX_LEN, D), 1.0)
    c_ctx = nrm((D,), 1.0)
    ada_w = nrm((DEPTH, D, 6 * D), D ** -0.5)
    ada_b = nrm((DEPTH, 6 * D), 0.02)
    w_in = nrm((DEPTH, D, IN_COLS), D ** -0.5)
    lru_conv_w = nrm((DEPTH, CONV_WIDTH, LRU_WIDTH), CONV_WIDTH ** -0.5)
    lru_conv_b = nrm((DEPTH, LRU_WIDTH), 0.02)
    lru_wa = nrm((DEPTH, N_DIR, LRU_HEADS, LRU_HEAD_DIM, LRU_HEAD_DIM), LRU_HEAD_DIM ** -0.5)
    lru_ba = nrm((DEPTH, N_DIR, LRU_WIDTH), 0.02)
    lru_wx = nrm((DEPTH, N_DIR, LRU_HEADS, LRU_HEAD_DIM, LRU_HEAD_DIM), LRU_HEAD_DIM ** -0.5)
    lru_bx = nrm((DEPTH, N_DIR, LRU_WIDTH), 0.02)
    a_base = uni((DEPTH, N_DIR, LRU_WIDTH), 0.9, 0.999) ** (1.0 / LRU_C)
    lru_lambda = jnp.log(a_base) - jnp.log1p(-a_base)
    rwkv_mu = uni((DEPTH, 2, RWKV_COLS), 0.0, 0.5)
    rwkv_w0 = uni((DEPTH, N_DIR, RWKV_WIDTH), -6.0, -1.0)
    rwkv_w2 = nrm((DEPTH, N_DIR, DECAY_LORA, RWKV_WIDTH), 0.5 * DECAY_LORA ** -0.5)
    rwkv_a0 = nrm((DEPTH, N_DIR, RWKV_WIDTH), 0.1)
    rwkv_a2 = nrm((DEPTH, N_DIR, AAA_LORA, RWKV_WIDTH), AAA_LORA ** -0.5)
    rwkv_g2 = nrm((DEPTH, GATE_LORA, RWKV_WIDTH), GATE_LORA ** -0.5)
    rwkv_k_k = 0.85 + nrm((DEPTH, RWKV_WIDTH), 0.02)
    rwkv_k_a = 1.0 + nrm((DEPTH, RWKV_WIDTH), 0.02)
    rwkv_r_k = nrm((DEPTH, RWKV_HEADS, RWKV_HEAD_DIM), 0.1)
    rwkv_gn_g = 1.0 + nrm((DEPTH, RWKV_WIDTH), 0.05)
    rwkv_gn_b = nrm((DEPTH, RWKV_WIDTH), 0.02)
    w_out = nrm((DEPTH, MIX_WIDTH, D), BETA * MIX_WIDTH ** -0.5)
    ln1_g = 1.0 + nrm((DEPTH, D), 0.05)
    ln1_b = nrm((DEPTH, D), 0.02)
    peer_wq = nrm((DEPTH, D, PEER_HEADS * PEER_D_QUERY), D ** -0.5)
    peer_subkeys = nrm((DEPTH, PEER_HEADS, 2, PEER_N_KEYS, PEER_D_QUERY // 2), (PEER_D_QUERY // 2) ** -0.5)
    peer_u = nrm((DEPTH, PEER_N_EXPERTS, D), D ** -0.5)
    peer_v = nrm((DEPTH, PEER_N_EXPERTS, D), BETA)
    ln2_g = 1.0 + nrm((DEPTH, D), 0.05)
    ln2_b = nrm((DEPTH, D), 0.02)
    return {'x': x, 'c': c, 'ctx': ctx, 'c_ctx': c_ctx, 'ada_w': ada_w, 'ada_b': ada_b,
            'w_in': w_in, 'lru_conv_w': lru_conv_w, 'lru_conv_b': lru_conv_b,
            'lru_wa': lru_wa, 'lru_ba': lru_ba, 'lru_wx': lru_wx, 'lru_bx': lru_bx,
            'lru_lambda': lru_lambda, 'rwkv_mu': rwkv_mu, 'rwkv_w0': rwkv_w0, 'rwkv_w2': rwkv_w2,
            'rwkv_a0': rwkv_a0, 'rwkv_a2': rwkv_a2, 'rwkv_g2': rwkv_g2, 'rwkv_k_k': rwkv_k_k,
            'rwkv_k_a': rwkv_k_a, 'rwkv_r_k': rwkv_r_k, 'rwkv_gn_g': rwkv_gn_g, 'rwkv_gn_b': rwkv_gn_b,
            'w_out': w_out, 'ln1_g': ln1_g, 'ln1_b': ln1_b, 'peer_wq': peer_wq,
            'peer_subkeys': peer_subkeys, 'peer_u': peer_u, 'peer_v': peer_v,
            'ln2_g': ln2_g, 'ln2_b': ln2_b}


def reference(x, c, ctx, c_ctx, ada_w, ada_b, w_in, lru_conv_w, lru_conv_b, lru_wa, lru_ba,
              lru_wx, lru_bx, lru_lambda, rwkv_mu, rwkv_w0, rwkv_w2, rwkv_a0, rwkv_a2, rwkv_g2,
              rwkv_k_k, rwkv_k_a, rwkv_r_k, rwkv_gn_g, rwkv_gn_b, w_out, ln1_g, ln1_b,
              peer_wq, peer_subkeys, peer_u, peer_v, ln2_g, ln2_b):
    rows = x.shape[1] // GRID_W
    x = x + grid_pos_embed(rows, D_MODEL).astype(x.dtype)[None]
    n_ctx = ctx.shape[1]
    c_act = jax.nn.silu(c)
    cc_act = jax.nn.silu(c_ctx)
    for l in range(DEPTH):
        keep_ctx = l < DEPTH - 1
        mod_lat = (c_act @ ada_w[l] + ada_b[l])[:, None, :]
        mod_ctx = (cc_act @ ada_w[l] + ada_b[l])[None, None, :]
        sh1, sc1, gt1, sh2, sc2, gt2 = jnp.split(mod_lat, 6, axis=-1)
        csh1, csc1, cgt1, csh2, csc2, cgt2 = jnp.split(mod_ctx, 6, axis=-1)
        o_ctx, o_lat = token_mixer(
            modulate(ctx, csh1, csc1), modulate(x, sh1, sc1), keep_ctx, w_in[l],
            lru_conv_w[l], lru_conv_b[l], lru_wa[l], lru_ba[l], lru_wx[l], lru_bx[l], lru_lambda[l],
            rwkv_mu[l], rwkv_w0[l], rwkv_w2[l], rwkv_a0[l], rwkv_a2[l], rwkv_g2[l],
            rwkv_k_k[l], rwkv_k_a[l], rwkv_r_k[l], rwkv_gn_g[l], rwkv_gn_b[l], w_out[l])
        x = layer_norm(ALPHA * x + gt1 * o_lat, ln1_g[l], ln1_b[l])
        if keep_ctx:
            ctx = layer_norm(ALPHA * ctx + cgt1 * o_ctx, ln1_g[l], ln1_b[l])
            h2 = jnp.concatenate([modulate(ctx, csh2, csc2), modulate(x, sh2, sc2)], axis=1)
            f = peer(h2, peer_wq[l], peer_subkeys[l], peer_u[l], peer_v[l])
            ctx = layer_norm(ALPHA * ctx + cgt2 * f[:, :n_ctx], ln2_g[l], ln2_b[l])
            f_lat = f[:, n_ctx:]
        else:
            f_lat = peer(modulate(x, sh2, sc2), peer_wq[l], peer_subkeys[l], peer_u[l], peer_v[l])
        x = layer_norm(ALPHA * x + gt2 * f_lat, ln2_g[l], ln2_b[l])
    return x
```

```python
import jax
import jax.numpy as jnp
from jax import lax
from jax.experimental import pallas as pl
from jax.experimental.pallas import tpu as pltpu

F32 = jnp.float32
BF16 = jnp.bfloat16
HI = lax.Precision.HIGHEST

GRID_W = 64
N_DIR = 2
LRU_WIDTH = 1024
LRU_C = 8.0
CONV_WIDTH = 4
RWKV_WIDTH = 1024
HEAD_DIM = 64
DECAY_LORA = 64
AAA_LORA = 64
GATE_LORA = 160
PEER_HEADS = 8
PEER_N_KEYS = 128
PEER_TOPK = 16
LN_EPS = 1e-5
GN_EPS = 64e-5

LANES = 128
PAIR = LANES
CHUNK = 64
VMEM_LIMIT = 56 * 1024 * 1024


def _pick_block(n, target, align):
    best = None
    for b in range(align, min(n, target) + 1, align):
        if n % b == 0:
            best = b
    return best if best is not None else n


def _dot_nt(a, b):
    return lax.dot_general(a, b, (((1,), (1,)), ((), ())), preferred_element_type=F32)


def _dot_tn(a, b):
    return lax.dot_general(a, b, (((0,), (0,)), ((), ())), preferred_element_type=F32)


def _dot(a, b):
    return jnp.dot(a, b, preferred_element_type=F32)


def _dot_hi(a, b):
    return jnp.dot(a, b, precision=HI, preferred_element_type=F32)


def _softplus(x):
    return jnp.maximum(x, 0.0) + jnp.log1p(jnp.exp(-jnp.abs(x)))


def _mm_body(x_ref, w_ref, o_ref):
    o_ref[...] = _dot(x_ref[...].astype(BF16), w_ref[...].astype(BF16)).astype(o_ref.dtype)


def matmul(x, w, bm_target=1088, bn_target=512):
    M, K = x.shape
    N = w.shape[1]
    bm = _pick_block(M, bm_target, 8)
    bn = _pick_block(N, bn_target, LANES)
    return pl.pallas_call(
        _mm_body,
        grid=(M // bm, N // bn),
        in_specs=[pl.BlockSpec((bm, K), lambda i, j: (i, 0)),
                  pl.BlockSpec((K, bn), lambda i, j: (0, j))],
        out_specs=pl.BlockSpec((bm, bn), lambda i, j: (i, j)),
        out_shape=jax.ShapeDtypeStruct((M, N), F32),
        compiler_params=pltpu.CompilerParams(
            dimension_semantics=("parallel", "parallel"), vmem_limit_bytes=VMEM_LIMIT),
        name="matmul",
    )(x, w)


def _lru_body(xc_ref, w_ref, ba_ref, bx_ref, lam_ref, y_ref, a_scr, b_scr, h_scr):
    tb = xc_ref.shape[1]
    n_pair = xc_ref.shape[2] // PAIR

    @pl.when(pl.program_id(1) == 0)
    def _():
        h_scr[...] = jnp.zeros_like(h_scr)

    xc = xc_ref[0]
    ra, ia = [], []
    for p in range(n_pair):
        g = _dot(xc[:, p * PAIR:(p + 1) * PAIR].astype(BF16), w_ref[0, p])
        ra.append(g[:, :PAIR])
        ia.append(g[:, PAIR:])
    r = jax.nn.sigmoid(jnp.concatenate(ra, axis=1) + ba_ref[0])
    i = jax.nn.sigmoid(jnp.concatenate(ia, axis=1) + bx_ref[0])
    log_a = -LRU_C * r * _softplus(-lam_ref[0])
    a_scr[...] = jnp.exp(log_a)
    b_scr[...] = jnp.sqrt(jnp.maximum(1.0 - jnp.exp(2.0 * log_a), 0.0)) * (i * xc)

    def step(t, h):
        h = a_scr[pl.ds(t, 1), :] * h + b_scr[pl.ds(t, 1), :]
        y_ref[0, pl.ds(t, 1), :] = h
        return h

    h_scr[...] = lax.fori_loop(0, tb, step, h_scr[...], unroll=8)


def lru_scan(xc_dir, w_bd, ba, bx, lam, n_batch):
    G, TT, C = xc_dir.shape
    tb = _pick_block(TT, 256, 8)
    dmap = lambda g, t: (g // n_batch, 0, 0)
    return pl.pallas_call(
        _lru_body,
        grid=(G, TT // tb),
        in_specs=[pl.BlockSpec((1, tb, C), lambda g, t: (g, t, 0)),
                  pl.BlockSpec((1, C // PAIR, PAIR, 2 * PAIR), lambda g, t: (g // n_batch, 0, 0, 0)),
                  pl.BlockSpec((1, 1, C), dmap),
                  pl.BlockSpec((1, 1, C), dmap),
                  pl.BlockSpec((1, 1, C), dmap)],
        out_specs=pl.BlockSpec((1, tb, C), lambda g, t: (g, t, 0)),
        out_shape=jax.ShapeDtypeStruct((G, TT, C), F32),
        scratch_shapes=[pltpu.VMEM((tb, C), F32), pltpu.VMEM((tb, C), F32), pltpu.VMEM((1, C), F32)],
        compiler_params=pltpu.CompilerParams(
            dimension_semantics=("parallel", "arbitrary"), vmem_limit_bytes=VMEM_LIMIT),
        name="lru_scan",
    )(xc_dir, w_bd, ba, bx, lam)


def _rwkv_body(r_ref, k_ref, v_ref, wd_ref, ad_ref, w0_ref, w2_ref, a0_ref, a2_ref,
               kk_ref, ka_ref, rk_ref, y_ref, bo_ref, s_ref):
    L = r_ref.shape[1]
    C = r_ref.shape[2]
    n_pair = C // PAIR

    @pl.when(pl.program_id(1) == 0)
    def _():
        s_ref[...] = jnp.zeros_like(s_ref)

    wpre = w0_ref[0] + _dot_hi(jnp.tanh(wd_ref[0]), w2_ref[0])
    ld = -jnp.exp(-_softplus(-wpre) - 0.5)
    alr = jax.nn.sigmoid(a0_ref[0] + _dot_hi(ad_ref[0], a2_ref[0]))

    ti = lax.broadcasted_iota(jnp.int32, (L, L), 0)
    tj = lax.broadcasted_iota(jnp.int32, (L, L), 1)
    cs = _dot_hi((ti >= tj).astype(F32), ld)

    ri = lax.broadcasted_iota(jnp.int32, (PAIR, PAIR), 0)
    ci = lax.broadcasted_iota(jnp.int32, (PAIR, PAIR), 1)
    same = (ri // HEAD_DIM) == (ci // HEAD_DIM)
    bd_ones = same.astype(F32)
    strict = same & (ri > ci)
    incl = same & (ri >= ci)
    eye = (ri == ci).astype(F32)
    m0 = lax.broadcasted_iota(jnp.int32, (L, PAIR), 1) < HEAD_DIM

    def dup(z):
        return jnp.concatenate([z, z], axis=0)

    def slab(z):
        return jnp.where(m0, z[:L], z[L:])

    for p in range(n_pair):
        sl = slice(p * PAIR, (p + 1) * PAIR)
        r = r_ref[0, :, sl]
        k = k_ref[0, :, sl]
        v = v_ref[0, :, sl]
        ld_p = ld[:, sl]
        alr_p = alr[:, sl]
        cs_p = cs[:, sl]

        kk0 = k * kk_ref[:, sl]
        ssq = _dot_hi(kk0 * kk0, bd_ones)
        kk = kk0 * lax.rsqrt(jnp.maximum(ssq, 1e-24))
        kd = k * (1.0 + (alr_p - 1.0) * ka_ref[:, sl])
        a = -kk
        b = kk * alr_p
        bo_ref[0, :, sl] = _dot_hi(r * kd * rk_ref[:, sl], bd_ones) * v

        cs_l = cs_p[L - 1:L, :]
        g_inv = jnp.exp(-cs_p)
        g_rel = jnp.exp(cs_l - cs_p)
        at = a * jnp.exp(cs_p - ld_p)
        rt = r * jnp.exp(cs_p)
        bt = b * g_inv
        kt = kd * g_inv
        bh = b * g_rel
        kh = kd * g_rel

        zero = jnp.zeros_like(at)
        lhs4 = jnp.concatenate([jnp.where(m0, at, zero), jnp.where(m0, zero, at),
                                jnp.where(m0, rt, zero), jnp.where(m0, zero, rt)], axis=0).astype(BF16)
        ab = _dot_nt(lhs4, dup(bt).astype(BF16))
        ak = _dot_nt(lhs4, dup(kt).astype(BF16))
        a_ab = jnp.where(strict, ab[:2 * L], 0.0)
        a_rb = jnp.where(incl, ab[2 * L:], 0.0)
        a_ak = jnp.where(strict, ak[:2 * L], 0.0)
        a_rk = jnp.where(incl, ak[2 * L:], 0.0)

        tinv = eye
        blk = 1
        while blk < L:
            lower_left = (same & ((ri // (2 * blk)) == (ci // (2 * blk)))
                          & ((ri % (2 * blk)) >= blk) & ((ci % (2 * blk)) < blk))
            off = jnp.where(lower_left, a_ab, 0.0).astype(BF16)
            tb = tinv.astype(BF16)
            tinv = tinv + _dot(_dot(tb, off).astype(BF16), tb)
            blk *= 2

        v2 = dup(v).astype(BF16)
        av = _dot(a_ak.astype(BF16), v2)
        x = _dot(tinv.astype(BF16), jnp.concatenate([dup(at), av], axis=1).astype(BF16))
        abar = slab(x[:, :PAIR])
        uv = slab(x[:, PAIR:])
        z = _dot(a_rb.astype(BF16), jnp.concatenate([dup(abar), dup(uv)], axis=1).astype(BF16))
        kv = _dot(a_rk.astype(BF16), v2)
        rbar = rt + slab(z[:, :PAIR])
        yv = slab(z[:, PAIR:] + kv)
        mp = jnp.where(same, _dot_tn(abar.astype(BF16), bh.astype(BF16)), 0.0)
        sv = jnp.where(same, _dot_tn(jnp.concatenate([uv, v], axis=0).astype(BF16),
                                     jnp.concatenate([bh, kh], axis=0).astype(BF16)), 0.0)

        s0 = s_ref[p]
        s0b = s0.astype(BF16)
        y_ref[0, :, sl] = _dot_nt(rbar.astype(BF16), s0b) + yv
        s_ref[p] = s0 * jnp.exp(cs_l) + _dot(s0b, mp.astype(BF16)) + sv


def rwkv_scan(rkv_dir, wd_dir, ad_dir, w0, w2, a0, a2, k_k, k_a, r_k, n_batch):
    G, TT, C3 = rkv_dir.shape
    C = C3 // 3
    L = CHUNK
    assert TT % L == 0 and 2 * L == PAIR and C % PAIR == 0
    dmap3 = lambda g, c: (g // n_batch, 0, 0)
    cmap = lambda g, c: (0, 0)
    seq = lambda j: pl.BlockSpec((1, L, C), lambda g, c: (g, c, j))
    lora = pl.BlockSpec((1, L, wd_dir.shape[2]), lambda g, c: (g, c, 0))
    return pl.pallas_call(
        _rwkv_body,
        grid=(G, TT // L),
        in_specs=[seq(0), seq(1), seq(2), lora, lora,
                  pl.BlockSpec((1, 1, C), dmap3), pl.BlockSpec((1, w2.shape[1], C), dmap3),
                  pl.BlockSpec((1, 1, C), dmap3), pl.BlockSpec((1, a2.shape[1], C), dmap3),
                  pl.BlockSpec((1, C), cmap), pl.BlockSpec((1, C), cmap), pl.BlockSpec((1, C), cmap)],
        out_specs=[pl.BlockSpec((1, L, C), lambda g, c: (g, c, 0)),
                   pl.BlockSpec((1, L, C), lambda g, c: (g, c, 0))],
        out_shape=[jax.ShapeDtypeStruct((G, TT, C), F32), jax.ShapeDtypeStruct((G, TT, C), F32)],
        scratch_shapes=[pltpu.VMEM((C // PAIR, PAIR, PAIR), F32)],
        compiler_params=pltpu.CompilerParams(
            dimension_semantics=("parallel", "arbitrary"), vmem_limit_bytes=VMEM_LIMIT),
        name="rwkv_scan",
    )(rkv_dir, rkv_dir, rkv_dir, wd_dir, ad_dir, w0, w2, a0, a2, k_k, k_a, r_k)


def _top_values(x, n):
    rows = lax.broadcasted_iota(jnp.int32, x.shape, 0)
    out = []
    for _ in range(n):
        m = jnp.max(x, axis=0, keepdims=True)
        out.append(m)
        first = jnp.min(jnp.where(x == m, rows, x.shape[0]), axis=0, keepdims=True)
        x = jnp.where(rows == first, -jnp.inf, x)
    return out


def _peer_score_body(h_ref, wq_ref, sk_ref, s1_ref, s2_ref, e1_ref, e2_ref, tau_ref):
    n_heads = sk_ref.shape[0]
    dq = sk_ref.shape[3]
    q = _dot(h_ref[...], wq_ref[...])
    taus = []
    for h in range(n_heads):
        s = []
        for half in range(2):
            o = (2 * h + half) * dq
            s.append(lax.dot_general(sk_ref[h, half], q[:, o:o + dq], (((1,), (1,)), ((), ())),
                                     precision=HI, preferred_element_type=F32))
        t1 = _top_values(s[0], PEER_TOPK)
        t2 = jnp.concatenate(_top_values(s[1], PEER_TOPK), axis=0)
        cand = jnp.concatenate([t + t2 for t in t1], axis=0)
        top = _top_values(cand, PEER_TOPK)
        zsum = jnp.exp(top[0] - top[0])
        for t in top[1:]:
            zsum = zsum + jnp.exp(t - top[0])
        s1_ref[h] = s[0]
        s2_ref[h] = s[1]
        e1_ref[h] = jnp.exp(s[0] - t1[0]) / zsum
        e2_ref[h] = jnp.exp(s[1] - t2[0:1])
        taus.append(top[-1])
    tau_ref[...] = jnp.concatenate(taus, axis=0)


def peer_scores(h2, wq, subkeys):
    M, D = h2.shape
    nh, _, nk, dq = subkeys.shape
    tb = _pick_block(M, 256, LANES)
    big = pl.BlockSpec((nh, nk, tb), lambda i: (0, 0, i))
    shp = jax.ShapeDtypeStruct((nh, nk, M), F32)
    return pl.pallas_call(
        _peer_score_body,
        grid=(M // tb,),
        in_specs=[pl.BlockSpec((tb, D), lambda i: (i, 0)),
                  pl.BlockSpec(wq.shape, lambda i: (0, 0)),
                  pl.BlockSpec(subkeys.shape, lambda i: (0, 0, 0, 0))],
        out_specs=[big, big, big, big, pl.BlockSpec((nh, tb), lambda i: (0, i))],
        out_shape=[shp, shp, shp, shp, jax.ShapeDtypeStruct((nh, M), F32)],
        compiler_params=pltpu.CompilerParams(
            dimension_semantics=("parallel",), vmem_limit_bytes=VMEM_LIMIT),
        name="peer_scores",
    )(h2, wq, subkeys)


def _peer_expert_body(h_ref, u_ref, v_ref, s1_ref, s2_ref, e1_ref, e2_ref, tau_ref, o_ref):
    j = pl.program_id(1)
    n_heads, nk, tb = s2_ref.shape
    eb = u_ref.shape[0]
    per = eb // nk

    @pl.when(j == 0)
    def _():
        o_ref[...] = jnp.zeros_like(o_ref)

    act = _dot_nt(u_ref[...], h_ref[...])
    act = 0.5 * act * (1.0 + lax.erf(act * (2.0 ** -0.5)))
    xs = []
    for c in range(per):
        e1 = j * per + c
        w = jnp.zeros((nk, tb), F32)
        for h in range(n_heads):
            sel = (s2_ref[h] + s1_ref[h, pl.ds(e1, 1), :]) >= tau_ref[h:h + 1, :]
            w = w + jnp.where(sel, e2_ref[h], 0.0) * e1_ref[h, pl.ds(e1, 1), :]
        xs.append((w * act[c * nk:(c + 1) * nk]).astype(BF16))
    x = jnp.concatenate(xs, axis=0) if per > 1 else xs[0]
    o_ref[...] += _dot_tn(x, v_ref[...])


def peer_experts(h2, u, v, s1, s2, e1, e2, tau):
    M, D = h2.shape
    nh, nk, _ = s1.shape
    E = u.shape[0]
    tb = _pick_block(M, 512, LANES)
    eb = _pick_block(E, 512, nk)
    big = pl.BlockSpec((nh, nk, tb), lambda i, j: (0, 0, i))
    return pl.pallas_call(
        _peer_expert_body,
        grid=(M // tb, E // eb),
        in_specs=[pl.BlockSpec((tb, D), lambda i, j: (i, 0)),
                  pl.BlockSpec((eb, D), lambda i, j: (j, 0)),
                  pl.BlockSpec((eb, D), lambda i, j: (j, 0)),
                  big, big, big, big,
                  pl.BlockSpec((nh, tb), lambda i, j: (0, i))],
        out_specs=pl.BlockSpec((tb, D), lambda i, j: (i, 0)),
        out_shape=jax.ShapeDtypeStruct((M, D), F32),
        compiler_params=pltpu.CompilerParams(
            dimension_semantics=("parallel", "arbitrary"), vmem_limit_bytes=VMEM_LIMIT),
        name="peer_experts",
    )(h2, u, v, s1, s2, e1, e2, tau)


def _layer_norm(z, g, b):
    mu = jnp.mean(z, -1, keepdims=True)
    var = jnp.mean(jnp.square(z - mu), -1, keepdims=True)
    return (z - mu) * lax.rsqrt(var + LN_EPS) * g + b


def _modulate(z, shift, scale):
    return z * (1.0 + scale) + shift


def _grid_pos_embed(rows, dim):
    t = jnp.arange(rows * GRID_W)
    row = (t // GRID_W).astype(F32)
    col = (t % GRID_W).astype(F32)
    quarter = dim // 4
    freq = 1.0 / (10000.0 ** (jnp.arange(quarter, dtype=F32) / quarter))

    def sincos(p):
        ang = p[:, None] * freq[None, :]
        return jnp.concatenate([jnp.sin(ang), jnp.cos(ang)], axis=-1)

    return jnp.concatenate([sincos(row), sincos(col)], axis=-1)


def _per_segment(fn, z, n_ctx):
    return jnp.concatenate([fn(z[:, :n_ctx]), fn(z[:, n_ctx:])], axis=1)


def _flipseg(z, n_ctx):
    return _per_segment(lambda s: s[:, ::-1], z, n_ctx)


def _dwconv(z, w, b):
    T = z.shape[1]
    left = CONV_WIDTH // 2
    zp = jnp.pad(z, ((0, 0), (left, CONV_WIDTH - 1 - left), (0, 0)))
    out = b
    for tap in range(CONV_WIDTH):
        out = out + zp[:, tap:tap + T] * w[tap]
    return out


def _token_shift(rw, mu):
    prev = jnp.pad(rw, ((0, 0), (1, 0), (0, 0)))[:, :-1]
    nxt = jnp.pad(rw, ((0, 0), (0, 1), (0, 0)))[:, 1:]
    return rw + mu[0] * (prev - rw) + mu[1] * (nxt - rw)


def _pair_block_diag(w):
    nh = w.shape[0]
    w = w.reshape(nh // 2, 2, HEAD_DIM, HEAD_DIM)
    z = jnp.zeros_like(w[:, 0])
    top = jnp.concatenate([w[:, 0], z], axis=2)
    bot = jnp.concatenate([z, w[:, 1]], axis=2)
    return jnp.concatenate([top, bot], axis=1)


def _token_mixer(h_all, n_ctx, keep_ctx, w_in, conv_w, conv_b, wa, ba, wx, bx, lam,
                 mu, w0, w2, a0, a2, g2, k_k, k_a, r_k, gn_g, gn_b, w_out):
    B, TT, D = h_all.shape
    C = LRU_WIDTH
    in_cols = w_in.shape[1]
    pad_cols = -in_cols % 512
    w_in_p = jnp.pad(w_in, ((0, 0), (0, pad_cols))).astype(BF16)
    p = matmul(h_all.astype(BF16).reshape(B * TT, D), w_in_p).reshape(B, TT, -1)

    xc = _per_segment(lambda s: _dwconv(s, conv_w, conv_b), p[..., :C], n_ctx)
    xc_dir = jnp.concatenate([xc, _flipseg(xc, n_ctx)], axis=0)
    w_bd = jnp.concatenate([_pair_block_diag(wa.reshape((-1,) + wa.shape[2:])),
                            _pair_block_diag(wx.reshape((-1,) + wx.shape[2:]))], axis=2)
    w_bd = w_bd.reshape(N_DIR, C // PAIR, PAIR, 2 * PAIR).astype(BF16)
    y = lru_scan(xc_dir, w_bd, ba[:, None, :], bx[:, None, :], lam[:, None, :], B)
    y_lru = y[:B] + _flipseg(y[B:], n_ctx)

    rw_cols = p.shape[-1] - 2 * C
    mu_p = jnp.pad(mu, ((0, 0), (0, rw_cols - mu.shape[1])))
    rw = _per_segment(lambda s: _token_shift(s, mu_p), p[..., 2 * C:], n_ctx)
    W = RWKV_WIDTH
    rkv = rw[..., :3 * W]
    o = 3 * W
    wd = rw[..., o:o + N_DIR * DECAY_LORA]
    o += N_DIR * DECAY_LORA
    ad = rw[..., o:o + N_DIR * AAA_LORA]
    o += N_DIR * AAA_LORA
    gd = rw[..., o:]
    rkv_dir = jnp.concatenate([rkv, _flipseg(rkv, n_ctx)], axis=0)
    wd_dir = jnp.concatenate([wd[..., :DECAY_LORA], _flipseg(wd[..., DECAY_LORA:], n_ctx)], axis=0)
    ad_dir = jnp.concatenate([ad[..., :AAA_LORA], _flipseg(ad[..., AAA_LORA:], n_ctx)], axis=0)
    yr, bo = rwkv_scan(rkv_dir, wd_dir, ad_dir, w0[:, None, :], w2, a0[:, None, :], a2,
                       k_k[None], k_a[None], r_k.reshape(1, -1), B)
    ys = yr[:B] + _flipseg(yr[B:], n_ctx)
    bonus = bo[:B] + _flipseg(bo[B:], n_ctx)
    yh = ys.reshape(B, TT, W // HEAD_DIM, HEAD_DIM)
    m = jnp.mean(yh, -1, keepdims=True)
    var = jnp.mean(jnp.square(yh - m), -1, keepdims=True)
    yn = ((yh - m) * lax.rsqrt(var + GN_EPS)).reshape(B, TT, W) * gn_g + gn_b
    g2_p = jnp.pad(g2, ((0, gd.shape[-1] - g2.shape[0]), (0, 0))).astype(BF16)
    gate_r = matmul(jax.nn.sigmoid(gd).astype(BF16).reshape(B * TT, -1), g2_p)
    y_rwkv = (yn + bonus) * gate_r.reshape(B, TT, W)

    gate_l = jax.nn.gelu(p[..., C:2 * C])
    merged = jnp.concatenate([gate_l * y_lru, y_rwkv], axis=-1)
    if not keep_ctx:
        merged = merged[:, n_ctx:]
    Bm, Tm, _ = merged.shape
    out = matmul(merged.astype(BF16).reshape(Bm * Tm, -1), w_out.astype(BF16))
    return out.reshape(Bm, Tm, D)


def _peer(h2, wq, subkeys, u, v):
    B, T2, D = h2.shape
    hb = h2.astype(BF16).reshape(B * T2, D)
    s1, s2, e1, e2, tau = peer_scores(hb, wq.astype(BF16), subkeys)
    f = peer_experts(hb, u.astype(BF16), v.astype(BF16), s1, s2, e1, e2, tau)
    return f.reshape(B, T2, D)


def _forward(x, c, ctx, c_ctx, ada_w, ada_b, w_in, lru_conv_w, lru_conv_b, lru_wa, lru_ba,
             lru_wx, lru_bx, lru_lambda, rwkv_mu, rwkv_w0, rwkv_w2, rwkv_a0, rwkv_a2, rwkv_g2,
             rwkv_k_k, rwkv_k_a, rwkv_r_k, rwkv_gn_g, rwkv_gn_b, w_out, ln1_g, ln1_b,
             peer_wq, peer_subkeys, peer_u, peer_v, ln2_g, ln2_b):
    B, T, D = x.shape
    depth = ada_w.shape[0]
    alpha = (2.0 * depth) ** 0.25
    n_ctx = ctx.shape[1]
    x = x + _grid_pos_embed(T // GRID_W, D).astype(x.dtype)[None]
    cin = jnp.concatenate([jax.nn.silu(c), jax.nn.silu(c_ctx)[None],
                           jnp.zeros((8 - B - 1, D), F32)], axis=0)
    for l in range(depth):
        keep_ctx = l < depth - 1
        mod = matmul(cin, ada_w[l]) + ada_b[l]
        sh1, sc1, gt1, sh2, sc2, gt2 = jnp.split(mod[:B, None, :], 6, axis=-1)
        csh1, csc1, cgt1, csh2, csc2, cgt2 = jnp.split(mod[B][None, None, :], 6, axis=-1)
        h_all = jnp.concatenate([_modulate(ctx, csh1, csc1), _modulate(x, sh1, sc1)], axis=1)
        o = _token_mixer(h_all, n_ctx, keep_ctx, w_in[l], lru_conv_w[l], lru_conv_b[l], lru_wa[l],
                         lru_ba[l], lru_wx[l], lru_bx[l], lru_lambda[l], rwkv_mu[l], rwkv_w0[l],
                         rwkv_w2[l], rwkv_a0[l], rwkv_a2[l], rwkv_g2[l], rwkv_k_k[l], rwkv_k_a[l],
                         rwkv_r_k[l], rwkv_gn_g[l], rwkv_gn_b[l], w_out[l])
        if keep_ctx:
            x = _layer_norm(alpha * x + gt1 * o[:, n_ctx:], ln1_g[l], ln1_b[l])
            ctx = _layer_norm(alpha * ctx + cgt1 * o[:, :n_ctx], ln1_g[l], ln1_b[l])
            h2 = jnp.concatenate([_modulate(ctx, csh2, csc2), _modulate(x, sh2, sc2)], axis=1)
            f = _peer(h2, peer_wq[l], peer_subkeys[l], peer_u[l], peer_v[l])
            ctx = _layer_norm(alpha * ctx + cgt2 * f[:, :n_ctx], ln2_g[l], ln2_b[l])
            f_lat = f[:, n_ctx:]
        else:
            x = _layer_norm(alpha * x + gt1 * o, ln1_g[l], ln1_b[l])
            f_lat = _peer(_modulate(x, sh2, sc2), peer_wq[l], peer_subkeys[l], peer_u[l], peer_v[l])
        x = _layer_norm(alpha * x + gt2 * f_lat, ln2_g[l], ln2_b[l])
    return x


def kernel(x, c, ctx, c_ctx, ada_w, ada_b, w_in, lru_conv_w, lru_conv_b, lru_wa, lru_ba, lru_wx, lru_bx, lru_lambda, rwkv_mu, rwkv_w0, rwkv_w2, rwkv_a0, rwkv_a2, rwkv_g2, rwkv_k_k, rwkv_k_a, rwkv_r_k, rwkv_gn_g, rwkv_gn_b, w_out, ln1_g, ln1_b, peer_wq, peer_subkeys, peer_u, peer_v, ln2_g, ln2_b):
    return _forward(x, c, ctx, c_ctx, ada_w, ada_b, w_in, lru_conv_w, lru_conv_b, lru_wa, lru_ba,
                    lru_wx, lru_bx, lru_lambda, rwkv_mu, rwkv_w0, rwkv_w2, rwkv_a0, rwkv_a2, rwkv_g2,
                    rwkv_k_k, rwkv_k_a, rwkv_r_k, rwkv_gn_g, rwkv_gn_b, w_out, ln1_g, ln1_b,
                    peer_wq, peer_subkeys, peer_u, peer_v, ln2_g, ln2_b)
```

```python
import functools
import math

import jax
import jax.numpy as jnp
from jax import lax
from jax.experimental import pallas as pl
from jax.experimental.pallas import tpu as pltpu

F32 = jnp.float32
BF16 = jnp.bfloat16
HI = lax.Precision.HIGHEST

GRID_W = 64
N_DIR = 2
LRU_WIDTH = 1024
LRU_C = 8.0
CONV_WIDTH = 4
RWKV_WIDTH = 1024
HEAD_DIM = 64
DECAY_LORA = 64
AAA_LORA = 64
GATE_LORA = 160
PEER_HEADS = 8
PEER_N_KEYS = 128
PEER_TOPK = 16
LN_EPS = 1e-5
GN_EPS = 64e-5

LANES = 128
PAIR = LANES
CHUNK = 64
VMEM_LIMIT = 56 * 1024 * 1024


def _pick_block(n, target, align):
    best = None
    for b in range(align, min(n, target) + 1, align):
        if n % b == 0:
            best = b
    return best if best is not None else n


def _dot_nt(a, b):
    return lax.dot_general(a, b, (((1,), (1,)), ((), ())), preferred_element_type=F32)


def _dot_tn(a, b):
    return lax.dot_general(a, b, (((0,), (0,)), ((), ())), preferred_element_type=F32)


def _dot(a, b):
    return jnp.dot(a, b, preferred_element_type=F32)


def _dot_hi(a, b):
    return jnp.dot(a, b, precision=HI, preferred_element_type=F32)


def _softplus(x):
    return jnp.maximum(x, 0.0) + jnp.log1p(jnp.exp(-jnp.abs(x)))


def _mm_body(x_ref, w_ref, o_ref):
    o_ref[...] = _dot(x_ref[...].astype(BF16), w_ref[...].astype(BF16)).astype(o_ref.dtype)


def matmul(x, w, bm_target=1088, bn_target=512):
    M, K = x.shape
    N = w.shape[1]
    bm = _pick_block(M, bm_target, 8)
    bn = _pick_block(N, bn_target, LANES)
    return pl.pallas_call(
        _mm_body,
        grid=(M // bm, N // bn),
        in_specs=[pl.BlockSpec((bm, K), lambda i, j: (i, 0)),
                  pl.BlockSpec((K, bn), lambda i, j: (0, j))],
        out_specs=pl.BlockSpec((bm, bn), lambda i, j: (i, j)),
        out_shape=jax.ShapeDtypeStruct((M, N), F32),
        compiler_params=pltpu.CompilerParams(
            dimension_semantics=("parallel", "parallel"), vmem_limit_bytes=VMEM_LIMIT),
        name="matmul",
    )(x, w)


def _scan_block(g, t, n_batch, nb_ctx, nb):
    bwd = jnp.where(t < nb_ctx, nb_ctx - 1 - t, nb - 1 - (t - nb_ctx))
    return jnp.where(g // n_batch == 1, bwd, t)


def _lru_body(xc_ref, w_ref, ba_ref, bx_ref, lam_ref, y_ref, a_scr, b_scr, h_scr, *, n_batch):
    tb = xc_ref.shape[1]
    n_pair = xc_ref.shape[2] // PAIR
    rev = pl.program_id(0) // n_batch == 1

    @pl.when(pl.program_id(1) == 0)
    def _():
        h_scr[...] = jnp.zeros_like(h_scr)

    xc = xc_ref[0]
    ra, ia = [], []
    for p in range(n_pair):
        g = _dot(xc[:, p * PAIR:(p + 1) * PAIR].astype(BF16), w_ref[0, p])
        ra.append(g[:, :PAIR])
        ia.append(g[:, PAIR:])
    r = jax.nn.sigmoid(jnp.concatenate(ra, axis=1) + ba_ref[0])
    i = jax.nn.sigmoid(jnp.concatenate(ia, axis=1) + bx_ref[0])
    log_a = -LRU_C * r * _softplus(-lam_ref[0])
    a_scr[...] = jnp.exp(log_a)
    b_scr[...] = jnp.sqrt(jnp.maximum(1.0 - jnp.exp(2.0 * log_a), 0.0)) * (i * xc)

    def step(t, h):
        tt = jnp.where(rev, tb - 1 - t, t)
        h = a_scr[pl.ds(tt, 1), :] * h + b_scr[pl.ds(tt, 1), :]
        y_ref[0, pl.ds(tt, 1), :] = h
        return h

    h_scr[...] = lax.fori_loop(0, tb, step, h_scr[...], unroll=8)


def lru_scan(xc, w_bd, ba, bx, lam, n_ctx):
    B, TT, C = xc.shape
    tb = _pick_block(math.gcd(n_ctx, TT - n_ctx), 256, 8)
    nb, nb_ctx = TT // tb, n_ctx // tb
    seq = lambda g, t: (g % B, _scan_block(g, t, B, nb_ctx, nb), 0)
    dmap = lambda g, t: (g // B, 0, 0)
    return pl.pallas_call(
        functools.partial(_lru_body, n_batch=B),
        grid=(N_DIR * B, nb),
        in_specs=[pl.BlockSpec((1, tb, C), seq),
                  pl.BlockSpec((1, C // PAIR, PAIR, 2 * PAIR), lambda g, t: (g // B, 0, 0, 0)),
                  pl.BlockSpec((1, 1, C), dmap),
                  pl.BlockSpec((1, 1, C), dmap),
                  pl.BlockSpec((1, 1, C), dmap)],
        out_specs=pl.BlockSpec((1, tb, C), lambda g, t: (g, _scan_block(g, t, B, nb_ctx, nb), 0)),
        out_shape=jax.ShapeDtypeStruct((N_DIR * B, TT, C), F32),
        scratch_shapes=[pltpu.VMEM((tb, C), F32), pltpu.VMEM((tb, C), F32), pltpu.VMEM((1, C), F32)],
        compiler_params=pltpu.CompilerParams(
            dimension_semantics=("parallel", "arbitrary"), vmem_limit_bytes=VMEM_LIMIT),
        name="lru_scan",
    )(xc, w_bd, ba, bx, lam)


def _rwkv_body(r_ref, k_ref, v_ref, wd_ref, ad_ref, w0_ref, w2_ref, a0_ref, a2_ref,
               kk_ref, ka_ref, rk_ref, y_ref, bo_ref, s_ref, *, n_batch):
    L = r_ref.shape[1]
    C = r_ref.shape[2]
    n_pair = C // PAIR
    pairs = range(n_pair)
    sgn = jnp.where(pl.program_id(0) // n_batch == 1, -1, 1)

    @pl.when(pl.program_id(1) == 0)
    def _():
        s_ref[...] = jnp.zeros_like(s_ref)

    wpre = w0_ref[0] + _dot_hi(jnp.tanh(wd_ref[0]), w2_ref[0])
    ld = -jnp.exp(-_softplus(-wpre) - 0.5)
    alr = jax.nn.sigmoid(a0_ref[0] + _dot_hi(ad_ref[0], a2_ref[0]))

    ti = lax.broadcasted_iota(jnp.int32, (L, L), 0)
    tj = lax.broadcasted_iota(jnp.int32, (L, L), 1)
    cs = _dot_hi((((ti - tj) * sgn) >= 0).astype(F32), ld)
    cs_end = jnp.where(sgn < 0, cs[0:1, :], cs[L - 1:L, :])

    ri = lax.broadcasted_iota(jnp.int32, (PAIR, PAIR), 0)
    ci = lax.broadcasted_iota(jnp.int32, (PAIR, PAIR), 1)
    same = (ri // HEAD_DIM) == (ci // HEAD_DIM)
    bd_ones = same.astype(F32)
    order = (ri - ci) * sgn
    strict = same & (order > 0)
    incl = same & (order >= 0)
    eye = (ri == ci).astype(F32)
    m0 = lax.broadcasted_iota(jnp.int32, (L, PAIR), 1) < HEAD_DIM

    def dup(z):
        return jnp.concatenate([z, z], axis=0)

    def slab(z):
        return jnp.where(m0, z[:L], z[L:])

    at, rt, v2, bh, kh, vv, lhs4, bt2, kt2, g_end = [], [], [], [], [], [], [], [], [], []
    for p in pairs:
        sl = slice(p * PAIR, (p + 1) * PAIR)
        r = r_ref[0, :, sl]
        k = k_ref[0, :, sl]
        v = v_ref[0, :, sl]
        ld_p, alr_p, cs_p, cs_l = ld[:, sl], alr[:, sl], cs[:, sl], cs_end[:, sl]
        kk0 = k * kk_ref[:, sl]
        ssq = _dot_hi(kk0 * kk0, bd_ones)
        kk = kk0 * lax.rsqrt(jnp.maximum(ssq, 1e-24))
        kd = k * (1.0 + (alr_p - 1.0) * ka_ref[:, sl])
        b = kk * alr_p
        bo_ref[0, :, sl] = _dot_hi(r * kd * rk_ref[:, sl], bd_ones) * v
        g_inv = jnp.exp(-cs_p)
        g_rel = jnp.exp(cs_l - cs_p)
        at_p = -kk * jnp.exp(cs_p - ld_p)
        rt_p = r * jnp.exp(cs_p)
        zero = jnp.zeros_like(at_p)
        lhs4.append(jnp.concatenate([jnp.where(m0, at_p, zero), jnp.where(m0, zero, at_p),
                                     jnp.where(m0, rt_p, zero), jnp.where(m0, zero, rt_p)], axis=0).astype(BF16))
        bt2.append(dup(b * g_inv).astype(BF16))
        kt2.append(dup(kd * g_inv).astype(BF16))
        at.append(at_p)
        rt.append(rt_p)
        vv.append(v)
        v2.append(dup(v).astype(BF16))
        bh.append((b * g_rel).astype(BF16))
        kh.append((kd * g_rel).astype(BF16))
        g_end.append(jnp.exp(cs_l))

    ab = [_dot_nt(lhs4[p], bt2[p]) for p in pairs]
    ak = [_dot_nt(lhs4[p], kt2[p]) for p in pairs]
    a_ab = [jnp.where(strict, ab[p][:2 * L], 0.0) for p in pairs]
    a_rb = [jnp.where(incl, ab[p][2 * L:], 0.0).astype(BF16) for p in pairs]
    a_ak = [jnp.where(strict, ak[p][:2 * L], 0.0).astype(BF16) for p in pairs]
    a_rk = [jnp.where(incl, ak[p][2 * L:], 0.0).astype(BF16) for p in pairs]
    av = [_dot(a_ak[p], v2[p]) for p in pairs]
    kv = [_dot(a_rk[p], v2[p]) for p in pairs]

    def off_diag(blk):
        rb = ((ri % (2 * blk)) >= blk).astype(jnp.int32)
        cb = ((ci % (2 * blk)) >= blk).astype(jnp.int32)
        return same & ((ri // (2 * blk)) == (ci // (2 * blk))) & (((rb - cb) * sgn) == 1)

    m1 = off_diag(1)
    tinv = [eye + jnp.where(m1, a_ab[p], 0.0) for p in pairs]
    blk = 2
    while blk < L:
        mb = off_diag(blk)
        tb = [tinv[p].astype(BF16) for p in pairs]
        t1 = [_dot(tb[p], jnp.where(mb, a_ab[p], 0.0).astype(BF16)).astype(BF16) for p in pairs]
        tinv = [tinv[p] + _dot(t1[p], tb[p]) for p in pairs]
        blk *= 2

    x = [_dot(tinv[p].astype(BF16), jnp.concatenate([dup(at[p]), av[p]], axis=1).astype(BF16)) for p in pairs]
    abar = [slab(x[p][:, :PAIR]) for p in pairs]
    uv = [slab(x[p][:, PAIR:]) for p in pairs]
    z = [_dot(a_rb[p], jnp.concatenate([dup(abar[p]), dup(uv[p])], axis=1).astype(BF16)) for p in pairs]
    rbar = [(rt[p] + slab(z[p][:, :PAIR])).astype(BF16) for p in pairs]
    yv = [slab(z[p][:, PAIR:] + kv[p]) for p in pairs]
    mp = [jnp.where(same, _dot_tn(abar[p].astype(BF16), bh[p]), 0.0).astype(BF16) for p in pairs]
    sv = [jnp.where(same, _dot_tn(jnp.concatenate([uv[p], vv[p]], axis=0).astype(BF16),
                                  jnp.concatenate([bh[p], kh[p]], axis=0)), 0.0) for p in pairs]
    for p in pairs:
        sl = slice(p * PAIR, (p + 1) * PAIR)
        s0 = s_ref[p]
        s0b = s0.astype(BF16)
        y_ref[0, :, sl] = _dot_nt(rbar[p], s0b) + yv[p]
        s_ref[p] = s0 * g_end[p] + _dot(s0b, mp[p]) + sv[p]


def rwkv_scan(rw, w0, w2p, a0, a2p, k_k, k_a, r_k, n_ctx):
    B, TT, _ = rw.shape
    C = RWKV_WIDTH
    L = CHUNK
    assert n_ctx % L == 0 and TT % L == 0 and 2 * L == PAIR and C % PAIR == 0
    nb, nb_ctx = TT // L, n_ctx // L
    dmap3 = lambda g, c: (g // B, 0, 0)
    cmap = lambda g, c: (0, 0)
    seq = lambda j: pl.BlockSpec((1, L, C), lambda g, c: (g % B, _scan_block(g, c, B, nb_ctx, nb), j))
    lora = lambda j: pl.BlockSpec((1, L, PAIR), lambda g, c: (g % B, _scan_block(g, c, B, nb_ctx, nb), j))
    out = pl.BlockSpec((1, L, C), lambda g, c: (g, _scan_block(g, c, B, nb_ctx, nb), 0))
    return pl.pallas_call(
        functools.partial(_rwkv_body, n_batch=B),
        grid=(N_DIR * B, nb),
        in_specs=[seq(0), seq(1), seq(2), lora(3 * C // PAIR), lora(3 * C // PAIR + 1),
                  pl.BlockSpec((1, 1, C), dmap3), pl.BlockSpec((1, PAIR, C), dmap3),
                  pl.BlockSpec((1, 1, C), dmap3), pl.BlockSpec((1, PAIR, C), dmap3),
                  pl.BlockSpec((1, C), cmap), pl.BlockSpec((1, C), cmap), pl.BlockSpec((1, C), cmap)],
        out_specs=[out, out],
        out_shape=[jax.ShapeDtypeStruct((N_DIR * B, TT, C), F32)] * 2,
        scratch_shapes=[pltpu.VMEM((C // PAIR, PAIR, PAIR), F32)],
        compiler_params=pltpu.CompilerParams(
            dimension_semantics=("parallel", "arbitrary"), vmem_limit_bytes=VMEM_LIMIT),
        name="rwkv_scan",
    )(rw, rw, rw, rw, rw, w0, w2p, a0, a2p, k_k, k_a, r_k)


def _top_values(x, n):
    rows = lax.broadcasted_iota(jnp.int32, x.shape, 0)
    out = []
    for _ in range(n):
        m = jnp.max(x, axis=0, keepdims=True)
        out.append(m)
        first = jnp.min(jnp.where(x == m, rows, x.shape[0]), axis=0, keepdims=True)
        x = jnp.where(rows == first, -jnp.inf, x)
    return out


def _peer_score_body(h_ref, wq_ref, sk_ref, s1_ref, s2_ref, e1_ref, e2_ref, tau_ref):
    n_heads = sk_ref.shape[0]
    dq = sk_ref.shape[3]
    q = _dot(h_ref[...], wq_ref[...])
    taus = []
    for h in range(n_heads):
        s = []
        for half in range(2):
            o = (2 * h + half) * dq
            s.append(lax.dot_general(sk_ref[h, half], q[:, o:o + dq], (((1,), (1,)), ((), ())),
                                     precision=HI, preferred_element_type=F32))
        t1 = _top_values(s[0], PEER_TOPK)
        t2 = jnp.concatenate(_top_values(s[1], PEER_TOPK), axis=0)
        cand = jnp.concatenate([t + t2 for t in t1], axis=0)
        top = _top_values(cand, PEER_TOPK)
        zsum = jnp.exp(top[0] - top[0])
        for t in top[1:]:
            zsum = zsum + jnp.exp(t - top[0])
        s1_ref[h] = s[0]
        s2_ref[h] = s[1]
        e1_ref[h] = jnp.exp(s[0] - t1[0]) / zsum
        e2_ref[h] = jnp.exp(s[1] - t2[0:1])
        taus.append(top[-1])
    tau_ref[...] = jnp.concatenate(taus, axis=0)


def peer_scores(h2, wq, subkeys):
    M, D = h2.shape
    nh, _, nk, dq = subkeys.shape
    tb = _pick_block(M, 256, LANES)
    big = pl.BlockSpec((nh, nk, tb), lambda i: (0, 0, i))
    shp = jax.ShapeDtypeStruct((nh, nk, M), F32)
    return pl.pallas_call(
        _peer_score_body,
        grid=(M // tb,),
        in_specs=[pl.BlockSpec((tb, D), lambda i: (i, 0)),
                  pl.BlockSpec(wq.shape, lambda i: (0, 0)),
                  pl.BlockSpec(subkeys.shape, lambda i: (0, 0, 0, 0))],
        out_specs=[big, big, big, big, pl.BlockSpec((nh, tb), lambda i: (0, i))],
        out_shape=[shp, shp, shp, shp, jax.ShapeDtypeStruct((nh, M), F32)],
        compiler_params=pltpu.CompilerParams(
            dimension_semantics=("parallel",), vmem_limit_bytes=VMEM_LIMIT),
        name="peer_scores",
    )(h2, wq, subkeys)


def _peer_expert_body(h_ref, u_ref, v_ref, s1_ref, s2_ref, e1_ref, e2_ref, tau_ref, o_ref):
    j = pl.program_id(1)
    n_heads, nk, tb = s2_ref.shape
    eb = u_ref.shape[0]
    per = eb // nk

    @pl.when(j == 0)
    def _():
        o_ref[...] = jnp.zeros_like(o_ref)

    act = _dot_nt(u_ref[...], h_ref[...])
    act = 0.5 * act * (1.0 + lax.erf(act * (2.0 ** -0.5)))
    xs = []
    for c in range(per):
        e1 = j * per + c
        w = jnp.zeros((nk, tb), F32)
        for h in range(n_heads):
            sel = (s2_ref[h] + s1_ref[h, pl.ds(e1, 1), :]) >= tau_ref[h:h + 1, :]
            w = w + jnp.where(sel, e2_ref[h], 0.0) * e1_ref[h, pl.ds(e1, 1), :]
        xs.append((w * act[c * nk:(c + 1) * nk]).astype(BF16))
    x = jnp.concatenate(xs, axis=0) if per > 1 else xs[0]
    o_ref[...] += _dot_tn(x, v_ref[...])


def peer_experts(h2, u, v, s1, s2, e1, e2, tau):
    M, D = h2.shape
    nh, nk, _ = s1.shape
    E = u.shape[0]
    tb = _pick_block(M, 512, LANES)
    eb = _pick_block(E, 512, nk)
    big = pl.BlockSpec((nh, nk, tb), lambda i, j: (0, 0, i))
    return pl.pallas_call(
        _peer_expert_body,
        grid=(M // tb, E // eb),
        in_specs=[pl.BlockSpec((tb, D), lambda i, j: (i, 0)),
                  pl.BlockSpec((eb, D), lambda i, j: (j, 0)),
                  pl.BlockSpec((eb, D), lambda i, j: (j, 0)),
                  big, big, big, big,
                  pl.BlockSpec((nh, tb), lambda i, j: (0, i))],
        out_specs=pl.BlockSpec((tb, D), lambda i, j: (i, 0)),
        out_shape=jax.ShapeDtypeStruct((M, D), F32),
        compiler_params=pltpu.CompilerParams(
            dimension_semantics=("parallel", "arbitrary"), vmem_limit_bytes=VMEM_LIMIT),
        name="peer_experts",
    )(h2, u, v, s1, s2, e1, e2, tau)


def _layer_norm(z, g, b):
    mu = jnp.mean(z, -1, keepdims=True)
    var = jnp.mean(jnp.square(z - mu), -1, keepdims=True)
    return (z - mu) * lax.rsqrt(var + LN_EPS) * g + b


def _modulate(z, shift, scale):
    return z * (1.0 + scale) + shift


def _grid_pos_embed(rows, dim):
    t = jnp.arange(rows * GRID_W)
    row = (t // GRID_W).astype(F32)
    col = (t % GRID_W).astype(F32)
    quarter = dim // 4
    freq = 1.0 / (10000.0 ** (jnp.arange(quarter, dtype=F32) / quarter))

    def sincos(p):
        ang = p[:, None] * freq[None, :]
        return jnp.concatenate([jnp.sin(ang), jnp.cos(ang)], axis=-1)

    return jnp.concatenate([sincos(row), sincos(col)], axis=-1)


def _per_segment(fn, z, n_ctx):
    return jnp.concatenate([fn(z[:, :n_ctx]), fn(z[:, n_ctx:])], axis=1)


def _dwconv(z, w, b):
    T = z.shape[1]
    left = CONV_WIDTH // 2
    zp = jnp.pad(z, ((0, 0), (left, CONV_WIDTH - 1 - left), (0, 0)))
    out = b
    for tap in range(CONV_WIDTH):
        out = out + zp[:, tap:tap + T] * w[tap]
    return out


def _token_shift(rw, mu):
    prev = jnp.pad(rw, ((0, 0), (1, 0), (0, 0)))[:, :-1]
    nxt = jnp.pad(rw, ((0, 0), (0, 1), (0, 0)))[:, 1:]
    return rw + mu[0] * (prev - rw) + mu[1] * (nxt - rw)


def _pair_block_diag(w):
    nh = w.shape[0]
    w = w.reshape(nh // 2, 2, HEAD_DIM, HEAD_DIM)
    z = jnp.zeros_like(w[:, 0])
    top = jnp.concatenate([w[:, 0], z], axis=2)
    bot = jnp.concatenate([z, w[:, 1]], axis=2)
    return jnp.concatenate([top, bot], axis=1)


def _token_mixer(h_all, n_ctx, keep_ctx, w_in, conv_w, conv_b, wa, ba, wx, bx, lam,
                 mu, w0, w2, a0, a2, g2, k_k, k_a, r_k, gn_g, gn_b, w_out):
    B, TT, D = h_all.shape
    C = LRU_WIDTH
    in_cols = w_in.shape[1]
    pad_cols = -in_cols % 512
    w_in_p = jnp.pad(w_in, ((0, 0), (0, pad_cols))).astype(BF16)
    p = matmul(h_all.astype(BF16).reshape(B * TT, D), w_in_p).reshape(B, TT, -1)

    xc = _per_segment(lambda s: _dwconv(s, conv_w, conv_b), p[..., :C], n_ctx)
    w_bd = jnp.concatenate([_pair_block_diag(wa.reshape((-1,) + wa.shape[2:])),
                            _pair_block_diag(wx.reshape((-1,) + wx.shape[2:]))], axis=2)
    w_bd = w_bd.reshape(N_DIR, C // PAIR, PAIR, 2 * PAIR).astype(BF16)
    y = lru_scan(xc, w_bd, ba[:, None, :], bx[:, None, :], lam[:, None, :], n_ctx)
    y_lru = y[:B] + y[B:]

    rw_cols = p.shape[-1] - 2 * C
    mu_p = jnp.pad(mu, ((0, 0), (0, rw_cols - mu.shape[1])))
    rw = _per_segment(lambda s: _token_shift(s, mu_p), p[..., 2 * C:], n_ctx)
    W = RWKV_WIDTH
    zl = jnp.zeros_like(w2[0])
    w2p = jnp.stack([jnp.concatenate([w2[0], zl], axis=0), jnp.concatenate([zl, w2[1]], axis=0)])
    a2p = jnp.stack([jnp.concatenate([a2[0], zl], axis=0), jnp.concatenate([zl, a2[1]], axis=0)])
    yr, bo = rwkv_scan(rw, w0[:, None, :], w2p, a0[:, None, :], a2p,
                       k_k[None], k_a[None], r_k.reshape(1, -1), n_ctx)
    gd = rw[..., 3 * W + N_DIR * (DECAY_LORA + AAA_LORA):]
    ys = yr[:B] + yr[B:]
    bonus = bo[:B] + bo[B:]
    yh = ys.reshape(B, TT, W // HEAD_DIM, HEAD_DIM)
    m = jnp.mean(yh, -1, keepdims=True)
    var = jnp.mean(jnp.square(yh - m), -1, keepdims=True)
    yn = ((yh - m) * lax.rsqrt(var + GN_EPS)).reshape(B, TT, W) * gn_g + gn_b
    g2_p = jnp.pad(g2, ((0, gd.shape[-1] - g2.shape[0]), (0, 0))).astype(BF16)
    gate_r = matmul(jax.nn.sigmoid(gd).astype(BF16).reshape(B * TT, -1), g2_p)
    y_rwkv = (yn + bonus) * gate_r.reshape(B, TT, W)

    gate_l = jax.nn.gelu(p[..., C:2 * C])
    merged = jnp.concatenate([gate_l * y_lru, y_rwkv], axis=-1)
    if not keep_ctx:
        merged = merged[:, n_ctx:]
    Bm, Tm, _ = merged.shape
    out = matmul(merged.astype(BF16).reshape(Bm * Tm, -1), w_out.astype(BF16))
    return out.reshape(Bm, Tm, D)


def _peer(h2, wq, subkeys, u, v):
    B, T2, D = h2.shape
    hb = h2.astype(BF16).reshape(B * T2, D)
    s1, s2, e1, e2, tau = peer_scores(hb, wq.astype(BF16), subkeys)
    f = peer_experts(hb, u.astype(BF16), v.astype(BF16), s1, s2, e1, e2, tau)
    return f.reshape(B, T2, D)


def _forward(x, c, ctx, c_ctx, ada_w, ada_b, w_in, lru_conv_w, lru_conv_b, lru_wa, lru_ba,
             lru_wx, lru_bx, lru_lambda, rwkv_mu, rwkv_w0, rwkv_w2, rwkv_a0, rwkv_a2, rwkv_g2,
             rwkv_k_k, rwkv_k_a, rwkv_r_k, rwkv_gn_g, rwkv_gn_b, w_out, ln1_g, ln1_b,
             peer_wq, peer_subkeys, peer_u, peer_v, ln2_g, ln2_b):
    B, T, D = x.shape
    depth = ada_w.shape[0]
    alpha = (2.0 * depth) ** 0.25
    n_ctx = ctx.shape[1]
    x = x + _grid_pos_embed(T // GRID_W, D).astype(x.dtype)[None]
    cin = jnp.concatenate([jax.nn.silu(c), jax.nn.silu(c_ctx)[None],
                           jnp.zeros((8 - B - 1, D), F32)], axis=0)
    for l in range(depth):
        keep_ctx = l < depth - 1
        mod = matmul(cin, ada_w[l]) + ada_b[l]
        sh1, sc1, gt1, sh2, sc2, gt2 = jnp.split(mod[:B, None, :], 6, axis=-1)
        csh1, csc1, cgt1, csh2, csc2, cgt2 = jnp.split(mod[B][None, None, :], 6, axis=-1)
        h_all = jnp.concatenate([_modulate(ctx, csh1, csc1), _modulate(x, sh1, sc1)], axis=1)
        o = _token_mixer(h_all, n_ctx, keep_ctx, w_in[l], lru_conv_w[l], lru_conv_b[l], lru_wa[l],
                         lru_ba[l], lru_wx[l], lru_bx[l], lru_lambda[l], rwkv_mu[l], rwkv_w0[l],
                         rwkv_w2[l], rwkv_a0[l], rwkv_a2[l], rwkv_g2[l], rwkv_k_k[l], rwkv_k_a[l],
                         rwkv_r_k[l], rwkv_gn_g[l], rwkv_gn_b[l], w_out[l])
        if keep_ctx:
            x = _layer_norm(alpha * x + gt1 * o[:, n_ctx:], ln1_g[l], ln1_b[l])
            ctx = _layer_norm(alpha * ctx + cgt1 * o[:, :n_ctx], ln1_g[l], ln1_b[l])
            h2 = jnp.concatenate([_modulate(ctx, csh2, csc2), _modulate(x, sh2, sc2)], axis=1)
            f = _peer(h2, peer_wq[l], peer_subkeys[l], peer_u[l], peer_v[l])
            ctx = _layer_norm(alpha * ctx + cgt2 * f[:, :n_ctx], ln2_g[l], ln2_b[l])
            f_lat = f[:, n_ctx:]
        else:
            x = _layer_norm(alpha * x + gt1 * o, ln1_g[l], ln1_b[l])
            f_lat = _peer(_modulate(x, sh2, sc2), peer_wq[l], peer_subkeys[l], peer_u[l], peer_v[l])
        x = _layer_norm(alpha * x + gt2 * f_lat, ln2_g[l], ln2_b[l])
    return x


def kernel(x, c, ctx, c_ctx, ada_w, ada_b, w_in, lru_conv_w, lru_conv_b, lru_wa, lru_ba, lru_wx, lru_bx, lru_lambda, rwkv_mu, rwkv_w0, rwkv_w2, rwkv_a0, rwkv_a2, rwkv_g2, rwkv_k_k, rwkv_k_a, rwkv_r_k, rwkv_gn_g, rwkv_gn_b, w_out, ln1_g, ln1_b, peer_wq, peer_subkeys, peer_u, peer_v, ln2_g, ln2_b):
    return _forward(x, c, ctx, c_ctx, ada_w, ada_b, w_in, lru_conv_w, lru_conv_b, lru_wa, lru_ba,
                    lru_wx, lru_bx, lru_lambda, rwkv_mu, rwkv_w0, rwkv_w2, rwkv_a0, rwkv_a2, rwkv_g2,
                    rwkv_k_k, rwkv_k_a, rwkv_r_k, rwkv_gn_g, rwkv_gn_b, w_out, ln1_g, ln1_b,
                    peer_wq, peer_subkeys, peer_u, peer_v, ln2_g, ln2_b)
```

```python
import functools
import math

import jax
import jax.numpy as jnp
from jax import lax
from jax.experimental import pallas as pl
from jax.experimental.pallas import tpu as pltpu

F32 = jnp.float32
BF16 = jnp.bfloat16
HI = lax.Precision.HIGHEST

GRID_W = 64
N_DIR = 2
LRU_WIDTH = 1024
LRU_C = 8.0
CONV_WIDTH = 4
RWKV_WIDTH = 1024
HEAD_DIM = 64
DECAY_LORA = 64
AAA_LORA = 64
GATE_LORA = 160
PEER_HEADS = 8
PEER_N_KEYS = 128
PEER_TOPK = 16
LN_EPS = 1e-5
GN_EPS = 64e-5

LANES = 128
SUBLANES = 8
PAIR = LANES
CHUNK = 64
VMEM_LIMIT = 56 * 1024 * 1024


def _pick_block(n, target, align):
    best = None
    for b in range(align, min(n, target) + 1, align):
        if n % b == 0:
            best = b
    return best if best is not None else n


def _dot_nt(a, b):
    return lax.dot_general(a, b, (((1,), (1,)), ((), ())), preferred_element_type=F32)


def _dot_tn(a, b):
    return lax.dot_general(a, b, (((0,), (0,)), ((), ())), preferred_element_type=F32)


def _dot(a, b):
    return jnp.dot(a, b, preferred_element_type=F32)


def _dot_hi(a, b):
    return jnp.dot(a, b, precision=HI, preferred_element_type=F32)


def _split(x, n):
    out = []
    for _ in range(n - 1):
        p = x.astype(BF16)
        out.append(p)
        x = x - p.astype(F32)
    out.append(x.astype(BF16))
    return out


def _dot_exact_rhs(a, b, n):
    b = b.astype(BF16)
    acc = None
    for p in _split(a, n):
        d = _dot(p, b)
        acc = d if acc is None else acc + d
    return acc


def _dot_exact_lhs(a, b, n):
    a = a.astype(BF16)
    acc = None
    for p in _split(b, n):
        d = _dot(a, p)
        acc = d if acc is None else acc + d
    return acc


def _dot_3pass(a, b):
    ah, al = _split(a, 2)
    bh, bl = _split(b, 2)
    return _dot(ah, bh) + (_dot(ah, bl) + _dot(al, bh))


def _softplus(x):
    return jnp.maximum(x, 0.0) + jnp.log1p(jnp.exp(-jnp.abs(x)))


def _mm_body(x_ref, w_ref, o_ref):
    o_ref[...] = _dot(x_ref[...].astype(BF16), w_ref[...].astype(BF16)).astype(o_ref.dtype)


def matmul(x, w, bm_target=1088, bn_target=512):
    M, K = x.shape
    N = w.shape[1]
    bm = _pick_block(M, bm_target, 8)
    bn = _pick_block(N, bn_target, LANES)
    return pl.pallas_call(
        _mm_body,
        grid=(M // bm, N // bn),
        in_specs=[pl.BlockSpec((bm, K), lambda i, j: (i, 0)),
                  pl.BlockSpec((K, bn), lambda i, j: (0, j))],
        out_specs=pl.BlockSpec((bm, bn), lambda i, j: (i, j)),
        out_shape=jax.ShapeDtypeStruct((M, N), F32),
        compiler_params=pltpu.CompilerParams(
            dimension_semantics=("parallel", "parallel"), vmem_limit_bytes=VMEM_LIMIT),
        name="matmul",
    )(x, w)


def _scan_block(g, t, n_batch, nb_ctx, nb):
    bwd = jnp.where(t < nb_ctx, nb_ctx - 1 - t, nb - 1 - (t - nb_ctx))
    return jnp.where(g // n_batch == 1, bwd, t)


def _lru_body(xc_ref, w_ref, ba_ref, bx_ref, lam_ref, y_ref, a_scr, b_scr, h_scr, *, n_batch):
    tb = xc_ref.shape[1]
    n_pair = xc_ref.shape[2] // PAIR
    rev = pl.program_id(0) // n_batch == 1

    @pl.when(pl.program_id(1) == 0)
    def _():
        h_scr[...] = jnp.zeros_like(h_scr)

    xc = xc_ref[0]
    ra, ia = [], []
    for p in range(n_pair):
        g = _dot(xc[:, p * PAIR:(p + 1) * PAIR].astype(BF16), w_ref[0, p])
        ra.append(g[:, :PAIR])
        ia.append(g[:, PAIR:])
    r = jax.nn.sigmoid(jnp.concatenate(ra, axis=1) + ba_ref[0])
    i = jax.nn.sigmoid(jnp.concatenate(ia, axis=1) + bx_ref[0])
    log_a = -LRU_C * r * _softplus(-lam_ref[0])
    a_scr[...] = jnp.exp(log_a)
    b_scr[...] = jnp.sqrt(jnp.maximum(1.0 - jnp.exp(2.0 * log_a), 0.0)) * (i * xc)

    def step(t, h):
        tt = jnp.where(rev, tb - 1 - t, t)
        h = a_scr[pl.ds(tt, 1), :] * h + b_scr[pl.ds(tt, 1), :]
        y_ref[0, pl.ds(tt, 1), :] = h
        return h

    h_scr[...] = lax.fori_loop(0, tb, step, h_scr[...], unroll=8)


def lru_scan(xc, w_bd, ba, bx, lam, n_ctx):
    B, TT, C = xc.shape
    tb = _pick_block(math.gcd(n_ctx, TT - n_ctx), 256, 8)
    nb, nb_ctx = TT // tb, n_ctx // tb
    seq = lambda g, t: (g % B, _scan_block(g, t, B, nb_ctx, nb), 0)
    dmap = lambda g, t: (g // B, 0, 0)
    return pl.pallas_call(
        functools.partial(_lru_body, n_batch=B),
        grid=(N_DIR * B, nb),
        in_specs=[pl.BlockSpec((1, tb, C), seq),
                  pl.BlockSpec((1, C // PAIR, PAIR, 2 * PAIR), lambda g, t: (g // B, 0, 0, 0)),
                  pl.BlockSpec((1, 1, C), dmap),
                  pl.BlockSpec((1, 1, C), dmap),
                  pl.BlockSpec((1, 1, C), dmap)],
        out_specs=pl.BlockSpec((1, tb, C), lambda g, t: (g, _scan_block(g, t, B, nb_ctx, nb), 0)),
        out_shape=jax.ShapeDtypeStruct((N_DIR * B, TT, C), F32),
        scratch_shapes=[pltpu.VMEM((tb, C), F32), pltpu.VMEM((tb, C), F32), pltpu.VMEM((1, C), F32)],
        compiler_params=pltpu.CompilerParams(
            dimension_semantics=("parallel", "arbitrary"), vmem_limit_bytes=VMEM_LIMIT),
        name="lru_scan",
    )(xc, w_bd, ba, bx, lam)


def _rwkv_body(r_ref, k_ref, v_ref, wd_ref, ad_ref, w0_ref, w2_ref, a0_ref, a2_ref,
               kk_ref, ka_ref, rk_ref, y_ref, bo_ref, s_ref, *, n_batch):
    L = r_ref.shape[1]
    C = r_ref.shape[2]
    n_pair = C // PAIR
    pairs = range(n_pair)
    sgn = jnp.where(pl.program_id(0) // n_batch == 1, -1, 1)

    @pl.when(pl.program_id(1) == 0)
    def _():
        s_ref[...] = jnp.zeros_like(s_ref)

    wpre = w0_ref[0] + _dot_3pass(jnp.tanh(wd_ref[0]), w2_ref[0])
    ld = -jnp.exp(-_softplus(-wpre) - 0.5)
    alr = jax.nn.sigmoid(a0_ref[0] + _dot_3pass(ad_ref[0], a2_ref[0]))

    ti = lax.broadcasted_iota(jnp.int32, (L, L), 0)
    tj = lax.broadcasted_iota(jnp.int32, (L, L), 1)
    cs = _dot_exact_lhs((((ti - tj) * sgn) >= 0).astype(F32), ld, 3)
    cs_end = jnp.where(sgn < 0, cs[0:1, :], cs[L - 1:L, :])

    ri = lax.broadcasted_iota(jnp.int32, (PAIR, PAIR), 0)
    ci = lax.broadcasted_iota(jnp.int32, (PAIR, PAIR), 1)
    same = (ri // HEAD_DIM) == (ci // HEAD_DIM)
    bd_ones = same.astype(F32)
    order = (ri - ci) * sgn
    strict = same & (order > 0)
    incl = same & (order >= 0)
    eye = (ri == ci).astype(F32)
    m0 = lax.broadcasted_iota(jnp.int32, (L, PAIR), 1) < HEAD_DIM

    def dup(z):
        return jnp.concatenate([z, z], axis=0)

    def slab(z):
        return jnp.where(m0, z[:L], z[L:])

    at, rt, v2, bh, kh, vv, lhs4, bt2, kt2, g_end = [], [], [], [], [], [], [], [], [], []
    for p in pairs:
        sl = slice(p * PAIR, (p + 1) * PAIR)
        r = r_ref[0, :, sl]
        k = k_ref[0, :, sl]
        v = v_ref[0, :, sl]
        ld_p, alr_p, cs_p, cs_l = ld[:, sl], alr[:, sl], cs[:, sl], cs_end[:, sl]
        kk0 = k * kk_ref[:, sl]
        ssq = _dot_exact_rhs(kk0 * kk0, bd_ones, 2)
        kk = kk0 * lax.rsqrt(jnp.maximum(ssq, 1e-24))
        kd = k * (1.0 + (alr_p - 1.0) * ka_ref[:, sl])
        b = kk * alr_p
        bo_ref[0, :, sl] = _dot_exact_rhs(r * kd * rk_ref[:, sl], bd_ones, 2) * v
        g_inv = jnp.exp(-cs_p)
        g_rel = jnp.exp(cs_l - cs_p)
        at_p = -kk * jnp.exp(cs_p - ld_p)
        rt_p = r * jnp.exp(cs_p)
        zero = jnp.zeros_like(at_p)
        lhs4.append(jnp.concatenate([jnp.where(m0, at_p, zero), jnp.where(m0, zero, at_p),
                                     jnp.where(m0, rt_p, zero), jnp.where(m0, zero, rt_p)], axis=0).astype(BF16))
        bt2.append(dup(b * g_inv).astype(BF16))
        kt2.append(dup(kd * g_inv).astype(BF16))
        at.append(at_p)
        rt.append(rt_p)
        vv.append(v)
        v2.append(dup(v).astype(BF16))
        bh.append((b * g_rel).astype(BF16))
        kh.append((kd * g_rel).astype(BF16))
        g_end.append(jnp.exp(cs_l))

    ab = [_dot_nt(lhs4[p], bt2[p]) for p in pairs]
    ak = [_dot_nt(lhs4[p], kt2[p]) for p in pairs]
    a_ab = [jnp.where(strict, ab[p][:2 * L], 0.0) for p in pairs]
    a_rb = [jnp.where(incl, ab[p][2 * L:], 0.0).astype(BF16) for p in pairs]
    a_ak = [jnp.where(strict, ak[p][:2 * L], 0.0).astype(BF16) for p in pairs]
    a_rk = [jnp.where(incl, ak[p][2 * L:], 0.0).astype(BF16) for p in pairs]
    av = [_dot(a_ak[p], v2[p]) for p in pairs]
    kv = [_dot(a_rk[p], v2[p]) for p in pairs]

    def off_diag(blk):
        rb = ((ri % (2 * blk)) >= blk).astype(jnp.int32)
        cb = ((ci % (2 * blk)) >= blk).astype(jnp.int32)
        return same & ((ri // (2 * blk)) == (ci // (2 * blk))) & (((rb - cb) * sgn) == 1)

    m1 = off_diag(1)
    tinv = [eye + jnp.where(m1, a_ab[p], 0.0) for p in pairs]
    blk = 2
    while blk < L:
        mb = off_diag(blk)
        tb = [tinv[p].astype(BF16) for p in pairs]
        t1 = [_dot(tb[p], jnp.where(mb, a_ab[p], 0.0).astype(BF16)).astype(BF16) for p in pairs]
        tinv = [tinv[p] + _dot(t1[p], tb[p]) for p in pairs]
        blk *= 2

    x = [_dot(tinv[p].astype(BF16), jnp.concatenate([dup(at[p]), av[p]], axis=1).astype(BF16)) for p in pairs]
    abar = [slab(x[p][:, :PAIR]) for p in pairs]
    uv = [slab(x[p][:, PAIR:]) for p in pairs]
    z = [_dot(a_rb[p], jnp.concatenate([dup(abar[p]), dup(uv[p])], axis=1).astype(BF16)) for p in pairs]
    rbar = [(rt[p] + slab(z[p][:, :PAIR])).astype(BF16) for p in pairs]
    yv = [slab(z[p][:, PAIR:] + kv[p]) for p in pairs]
    mp = [jnp.where(same, _dot_tn(abar[p].astype(BF16), bh[p]), 0.0).astype(BF16) for p in pairs]
    sv = [jnp.where(same, _dot_tn(jnp.concatenate([uv[p], vv[p]], axis=0).astype(BF16),
                                  jnp.concatenate([bh[p], kh[p]], axis=0)), 0.0) for p in pairs]
    for p in pairs:
        sl = slice(p * PAIR, (p + 1) * PAIR)
        s0 = s_ref[p]
        s0b = s0.astype(BF16)
        y_ref[0, :, sl] = _dot_nt(rbar[p], s0b) + yv[p]
        s_ref[p] = s0 * g_end[p] + _dot(s0b, mp[p]) + sv[p]


def rwkv_scan(rw, w0, w2p, a0, a2p, k_k, k_a, r_k, n_ctx):
    B, TT, _ = rw.shape
    C = RWKV_WIDTH
    L = CHUNK
    assert n_ctx % L == 0 and TT % L == 0 and 2 * L == PAIR and C % PAIR == 0
    nb, nb_ctx = TT // L, n_ctx // L
    dmap3 = lambda g, c: (g // B, 0, 0)
    cmap = lambda g, c: (0, 0)
    seq = lambda j: pl.BlockSpec((1, L, C), lambda g, c: (g % B, _scan_block(g, c, B, nb_ctx, nb), j))
    lora = lambda j: pl.BlockSpec((1, L, PAIR), lambda g, c: (g % B, _scan_block(g, c, B, nb_ctx, nb), j))
    out = pl.BlockSpec((1, L, C), lambda g, c: (g, _scan_block(g, c, B, nb_ctx, nb), 0))
    return pl.pallas_call(
        functools.partial(_rwkv_body, n_batch=B),
        grid=(N_DIR * B, nb),
        in_specs=[seq(0), seq(1), seq(2), lora(3 * C // PAIR), lora(3 * C // PAIR + 1),
                  pl.BlockSpec((1, 1, C), dmap3), pl.BlockSpec((1, PAIR, C), dmap3),
                  pl.BlockSpec((1, 1, C), dmap3), pl.BlockSpec((1, PAIR, C), dmap3),
                  pl.BlockSpec((1, C), cmap), pl.BlockSpec((1, C), cmap), pl.BlockSpec((1, C), cmap)],
        out_specs=[out, out],
        out_shape=[jax.ShapeDtypeStruct((N_DIR * B, TT, C), F32)] * 2,
        scratch_shapes=[pltpu.VMEM((C // PAIR, PAIR, PAIR), F32)],
        compiler_params=pltpu.CompilerParams(
            dimension_semantics=("parallel", "arbitrary"), vmem_limit_bytes=VMEM_LIMIT),
        name="rwkv_scan",
    )(rw, rw, rw, rw, rw, w0, w2p, a0, a2p, k_k, k_a, r_k)


def _top_values(x, n):
    rows = lax.broadcasted_iota(jnp.int32, x.shape, 0).astype(F32)
    out = []
    for _ in range(n):
        m = jnp.max(x, axis=0, keepdims=True)
        out.append(m)
        first = jnp.min(jnp.where(x == m, rows, float(x.shape[0])), axis=0, keepdims=True)
        x = jnp.where(rows == first, -jnp.inf, x)
    return out


def _candidate_rows(t1, t2, n):
    k = len(t1)
    t1c = jnp.concatenate(t1, axis=0)
    t2c = jnp.concatenate(t2, axis=0)
    row = lax.broadcasted_iota(jnp.int32, (SUBLANES, t1c.shape[1]), 0)
    groups = []
    for j in range(k):
        cnt = min(k, n // (j + 1))
        if cnt <= 1:
            break
        for g in range(0, cnt, SUBLANES):
            piece = t1c[g:g + SUBLANES] + t2[j]
            groups.append(piece if cnt - g >= SUBLANES else jnp.where(row < cnt - g, piece, -jnp.inf))
    j0 = j
    for g in range(j0, k, SUBLANES):
        piece = t1[0] + t2c[g:g + SUBLANES]
        groups.append(piece if k - g >= SUBLANES else jnp.where(row < k - g, piece, -jnp.inf))
    return jnp.concatenate(groups, axis=0)


def _peer_score_body(ht_ref, wqt_ref, sk_ref, t1_ref, s2_ref, e1_ref, e2_ref):
    n_heads = sk_ref.shape[0]
    dq = sk_ref.shape[3]
    qt = _dot(wqt_ref[...], ht_ref[...])
    for h in range(n_heads):
        s = [_dot_hi(sk_ref[h, half], qt[(2 * h + half) * dq:(2 * h + half + 1) * dq, :])
             for half in range(2)]
        t1 = _top_values(s[0], PEER_TOPK)
        t2 = _top_values(s[1], PEER_TOPK)
        top = _top_values(_candidate_rows(t1, t2, PEER_TOPK + 1), PEER_TOPK + 1)
        zsum = jnp.exp(top[0] - top[0])
        for t in top[1:PEER_TOPK]:
            zsum = zsum + jnp.exp(t - top[0])
        tau = 0.5 * (top[PEER_TOPK - 1] + top[PEER_TOPK])
        t1_ref[h] = jnp.where(s[0] >= t1[-1], tau - s[0], jnp.inf)
        s2_ref[h] = jnp.where(s[1] >= t2[-1], s[1], -jnp.inf)
        e1_ref[h] = jnp.exp(s[0] - t1[0]) / zsum
        e2_ref[h] = jnp.exp(s[1] - t2[0])


def peer_scores(ht, wqt, subkeys):
    D, M = ht.shape
    nh, _, nk, dq = subkeys.shape
    tb = _pick_block(M, 256, LANES)
    big = pl.BlockSpec((nh, nk, tb), lambda i: (0, 0, i))
    shp = jax.ShapeDtypeStruct((nh, nk, M), F32)
    return pl.pallas_call(
        _peer_score_body,
        grid=(M // tb,),
        in_specs=[pl.BlockSpec((D, tb), lambda i: (0, i)),
                  pl.BlockSpec(wqt.shape, lambda i: (0, 0)),
                  pl.BlockSpec(subkeys.shape, lambda i: (0, 0, 0, 0))],
        out_specs=[big, big, big, big],
        out_shape=[shp, shp, shp, shp],
        compiler_params=pltpu.CompilerParams(
            dimension_semantics=("parallel",), vmem_limit_bytes=VMEM_LIMIT),
        name="peer_scores",
    )(ht, wqt, subkeys)


E2_TILE = 64


def _peer_expert_body(ht_ref, u_ref, vt_ref, t1_ref, s2_ref, e1_ref, e2_ref, o_ref, act_scr, x_scr, ot_scr):
    j = pl.program_id(1)
    n_heads, nk, tb = s2_ref.shape
    eb = u_ref.shape[0]
    per = eb // nk

    @pl.when(j == 0)
    def _():
        ot_scr[...] = jnp.zeros_like(ot_scr)

    act_scr[...] = _dot(u_ref[...], ht_ref[...])
    t1full = [[t1_ref[h, pl.ds(j * per + c, 1), :] for h in range(n_heads)] for c in range(per)]
    e1full = [[e1_ref[h, pl.ds(j * per + c, 1), :] for h in range(n_heads)] for c in range(per)]
    for tj in range(tb // LANES):
        tsl = slice(tj * LANES, (tj + 1) * LANES)
        t1rows = [[r[:, tsl] for r in rows] for rows in t1full]
        e1rows = [[r[:, tsl] for r in rows] for rows in e1full]
        for et in range(nk // E2_TILE):
            esl = slice(et * E2_TILE, (et + 1) * E2_TILE)
            accs = [jnp.zeros((E2_TILE, LANES), F32) for _ in range(per)]
            for h in range(n_heads):
                s2t = s2_ref[h, esl, tsl]
                e2t = e2_ref[h, esl, tsl]
                for c in range(per):
                    accs[c] = accs[c] + jnp.where(s2t >= t1rows[c][h], e2t, 0.0) * e1rows[c][h]
            for c in range(per):
                rsl = slice(c * nk + et * E2_TILE, c * nk + (et + 1) * E2_TILE)
                a = act_scr[rsl, tsl]
                gelu = 0.5 * a * (1.0 + lax.erf(a * (2.0 ** -0.5)))
                x_scr[rsl, tsl] = (accs[c] * gelu).astype(BF16)
    ot_scr[...] += _dot(vt_ref[...], x_scr[...])

    @pl.when(j == pl.num_programs(1) - 1)
    def _():
        o_ref[...] = ot_scr[...].T


def peer_experts(ht, u, vt, t1, s2, e1, e2):
    D, M = ht.shape
    nh, nk, _ = s2.shape
    E = u.shape[0]
    tb = _pick_block(M, 512, LANES)
    eb = _pick_block(E, 512, nk)
    big = pl.BlockSpec((nh, nk, tb), lambda i, j: (0, 0, i))
    return pl.pallas_call(
        _peer_expert_body,
        grid=(M // tb, E // eb),
        in_specs=[pl.BlockSpec((D, tb), lambda i, j: (0, i)),
                  pl.BlockSpec((eb, D), lambda i, j: (j, 0)),
                  pl.BlockSpec((D, eb), lambda i, j: (0, j)),
                  big, big, big, big],
        out_specs=pl.BlockSpec((tb, D), lambda i, j: (i, 0)),
        out_shape=jax.ShapeDtypeStruct((M, D), F32),
        scratch_shapes=[pltpu.VMEM((eb, tb), F32), pltpu.VMEM((eb, tb), BF16), pltpu.VMEM((D, tb), F32)],
        compiler_params=pltpu.CompilerParams(
            dimension_semantics=("parallel", "arbitrary"), vmem_limit_bytes=VMEM_LIMIT),
        name="peer_experts",
    )(ht, u, vt, t1, s2, e1, e2)


def _layer_norm(z, g, b):
    mu = jnp.mean(z, -1, keepdims=True)
    var = jnp.mean(jnp.square(z - mu), -1, keepdims=True)
    return (z - mu) * lax.rsqrt(var + LN_EPS) * g + b


def _modulate(z, shift, scale):
    return z * (1.0 + scale) + shift


def _grid_pos_embed(rows, dim):
    t = jnp.arange(rows * GRID_W)
    row = (t // GRID_W).astype(F32)
    col = (t % GRID_W).astype(F32)
    quarter = dim // 4
    freq = 1.0 / (10000.0 ** (jnp.arange(quarter, dtype=F32) / quarter))

    def sincos(p):
        ang = p[:, None] * freq[None, :]
        return jnp.concatenate([jnp.sin(ang), jnp.cos(ang)], axis=-1)

    return jnp.concatenate([sincos(row), sincos(col)], axis=-1)


def _per_segment(fn, z, n_ctx):
    return jnp.concatenate([fn(z[:, :n_ctx]), fn(z[:, n_ctx:])], axis=1)


def _dwconv(z, w, b):
    T = z.shape[1]
    left = CONV_WIDTH // 2
    zp = jnp.pad(z, ((0, 0), (left, CONV_WIDTH - 1 - left), (0, 0)))
    out = b
    for tap in range(CONV_WIDTH):
        out = out + zp[:, tap:tap + T] * w[tap]
    return out


def _token_shift(rw, mu):
    prev = jnp.pad(rw, ((0, 0), (1, 0), (0, 0)))[:, :-1]
    nxt = jnp.pad(rw, ((0, 0), (0, 1), (0, 0)))[:, 1:]
    return rw + mu[0] * (prev - rw) + mu[1] * (nxt - rw)


def _pair_block_diag(w):
    nh = w.shape[0]
    w = w.reshape(nh // 2, 2, HEAD_DIM, HEAD_DIM)
    z = jnp.zeros_like(w[:, 0])
    top = jnp.concatenate([w[:, 0], z], axis=2)
    bot = jnp.concatenate([z, w[:, 1]], axis=2)
    return jnp.concatenate([top, bot], axis=1)


def _token_mixer(h_all, n_ctx, keep_ctx, w_in, conv_w, conv_b, wa, ba, wx, bx, lam,
                 mu, w0, w2, a0, a2, g2, k_k, k_a, r_k, gn_g, gn_b, w_out):
    B, TT, D = h_all.shape
    C = LRU_WIDTH
    in_cols = w_in.shape[1]
    pad_cols = -in_cols % 512
    w_in_p = jnp.pad(w_in, ((0, 0), (0, pad_cols))).astype(BF16)
    p = matmul(h_all.astype(BF16).reshape(B * TT, D), w_in_p).reshape(B, TT, -1)

    xc = _per_segment(lambda s: _dwconv(s, conv_w, conv_b), p[..., :C], n_ctx)
    w_bd = jnp.concatenate([_pair_block_diag(wa.reshape((-1,) + wa.shape[2:])),
                            _pair_block_diag(wx.reshape((-1,) + wx.shape[2:]))], axis=2)
    w_bd = w_bd.reshape(N_DIR, C // PAIR, PAIR, 2 * PAIR).astype(BF16)
    y = lru_scan(xc, w_bd, ba[:, None, :], bx[:, None, :], lam[:, None, :], n_ctx)
    y_lru = y[:B] + y[B:]

    rw_cols = p.shape[-1] - 2 * C
    mu_p = jnp.pad(mu, ((0, 0), (0, rw_cols - mu.shape[1])))
    rw = _per_segment(lambda s: _token_shift(s, mu_p), p[..., 2 * C:], n_ctx)
    W = RWKV_WIDTH
    zl = jnp.zeros_like(w2[0])
    w2p = jnp.stack([jnp.concatenate([w2[0], zl], axis=0), jnp.concatenate([zl, w2[1]], axis=0)])
    a2p = jnp.stack([jnp.concatenate([a2[0], zl], axis=0), jnp.concatenate([zl, a2[1]], axis=0)])
    yr, bo = rwkv_scan(rw, w0[:, None, :], w2p, a0[:, None, :], a2p,
                       k_k[None], k_a[None], r_k.reshape(1, -1), n_ctx)
    gd = rw[..., 3 * W + N_DIR * (DECAY_LORA + AAA_LORA):]
    ys = yr[:B] + yr[B:]
    bonus = bo[:B] + bo[B:]
    yh = ys.reshape(B, TT, W // HEAD_DIM, HEAD_DIM)
    m = jnp.mean(yh, -1, keepdims=True)
    var = jnp.mean(jnp.square(yh - m), -1, keepdims=True)
    yn = ((yh - m) * lax.rsqrt(var + GN_EPS)).reshape(B, TT, W) * gn_g + gn_b
    g2_p = jnp.pad(g2, ((0, gd.shape[-1] - g2.shape[0]), (0, 0))).astype(BF16)
    gate_r = matmul(jax.nn.sigmoid(gd).astype(BF16).reshape(B * TT, -1), g2_p)
    y_rwkv = (yn + bonus) * gate_r.reshape(B, TT, W)

    gate_l = jax.nn.gelu(p[..., C:2 * C])
    merged = jnp.concatenate([gate_l * y_lru, y_rwkv], axis=-1)
    if not keep_ctx:
        merged = merged[:, n_ctx:]
    Bm, Tm, _ = merged.shape
    out = matmul(merged.astype(BF16).reshape(Bm * Tm, -1), w_out.astype(BF16))
    return out.reshape(Bm, Tm, D)


def _peer(h2, wq, subkeys, u, v):
    B, T2, D = h2.shape
    ht = h2.astype(BF16).reshape(B * T2, D).T
    t1, s2, e1, e2 = peer_scores(ht, wq.T.astype(BF16), subkeys)
    f = peer_experts(ht, u.astype(BF16), v.T.astype(BF16), t1, s2, e1, e2)
    return f.reshape(B, T2, D)


def _forward(x, c, ctx, c_ctx, ada_w, ada_b, w_in, lru_conv_w, lru_conv_b, lru_wa, lru_ba,
             lru_wx, lru_bx, lru_lambda, rwkv_mu, rwkv_w0, rwkv_w2, rwkv_a0, rwkv_a2, rwkv_g2,
             rwkv_k_k, rwkv_k_a, rwkv_r_k, rwkv_gn_g, rwkv_gn_b, w_out, ln1_g, ln1_b,
             peer_wq, peer_subkeys, peer_u, peer_v, ln2_g, ln2_b):
    B, T, D = x.shape
    depth = ada_w.shape[0]
    alpha = (2.0 * depth) ** 0.25
    n_ctx = ctx.shape[1]
    x = x + _grid_pos_embed(T // GRID_W, D).astype(x.dtype)[None]
    cin = jnp.concatenate([jax.nn.silu(c), jax.nn.silu(c_ctx)[None],
                           jnp.zeros((8 - B - 1, D), F32)], axis=0)
    for l in range(depth):
        keep_ctx = l < depth - 1
        mod = matmul(cin, ada_w[l]) + ada_b[l]
        sh1, sc1, gt1, sh2, sc2, gt2 = jnp.split(mod[:B, None, :], 6, axis=-1)
        csh1, csc1, cgt1, csh2, csc2, cgt2 = jnp.split(mod[B][None, None, :], 6, axis=-1)
        h_all = jnp.concatenate([_modulate(ctx, csh1, csc1), _modulate(x, sh1, sc1)], axis=1)
        o = _token_mixer(h_all, n_ctx, keep_ctx, w_in[l], lru_conv_w[l], lru_conv_b[l], lru_wa[l],
                         lru_ba[l], lru_wx[l], lru_bx[l], lru_lambda[l], rwkv_mu[l], rwkv_w0[l],
                         rwkv_w2[l], rwkv_a0[l], rwkv_a2[l], rwkv_g2[l], rwkv_k_k[l], rwkv_k_a[l],
                         rwkv_r_k[l], rwkv_gn_g[l], rwkv_gn_b[l], w_out[l])
        if keep_ctx:
            x = _layer_norm(alpha * x + gt1 * o[:, n_ctx:], ln1_g[l], ln1_b[l])
            ctx = _layer_norm(alpha * ctx + cgt1 * o[:, :n_ctx], ln1_g[l], ln1_b[l])
            h2 = jnp.concatenate([_modulate(ctx, csh2, csc2), _modulate(x, sh2, sc2)], axis=1)
            f = _peer(h2, peer_wq[l], peer_subkeys[l], peer_u[l], peer_v[l])
            ctx = _layer_norm(alpha * ctx + cgt2 * f[:, :n_ctx], ln2_g[l], ln2_b[l])
            f_lat = f[:, n_ctx:]
        else:
            x = _layer_norm(alpha * x + gt1 * o, ln1_g[l], ln1_b[l])
            f_lat = _peer(_modulate(x, sh2, sc2), peer_wq[l], peer_subkeys[l], peer_u[l], peer_v[l])
        x = _layer_norm(alpha * x + gt2 * f_lat, ln2_g[l], ln2_b[l])
    return x


def kernel(x, c, ctx, c_ctx, ada_w, ada_b, w_in, lru_conv_w, lru_conv_b, lru_wa, lru_ba, lru_wx, lru_bx, lru_lambda, rwkv_mu, rwkv_w0, rwkv_w2, rwkv_a0, rwkv_a2, rwkv_g2, rwkv_k_k, rwkv_k_a, rwkv_r_k, rwkv_gn_g, rwkv_gn_b, w_out, ln1_g, ln1_b, peer_wq, peer_subkeys, peer_u, peer_v, ln2_g, ln2_b):
    return _forward(x, c, ctx, c_ctx, ada_w, ada_b, w_in, lru_conv_w, lru_conv_b, lru_wa, lru_ba,
                    lru_wx, lru_bx, lru_lambda, rwkv_mu, rwkv_w0, rwkv_w2, rwkv_a0, rwkv_a2, rwkv_g2,
                    rwkv_k_k, rwkv_k_a, rwkv_r_k, rwkv_gn_g, rwkv_gn_b, w_out, ln1_g, ln1_b,
                    peer_wq, peer_subkeys, peer_u, peer_v, ln2_g, ln2_b)
```

```python
import functools
import math

import jax
import jax.numpy as jnp
from jax import lax
from jax.experimental import pallas as pl
from jax.experimental.pallas import tpu as pltpu

F32 = jnp.float32
BF16 = jnp.bfloat16
HI = lax.Precision.HIGHEST

GRID_W = 64
N_DIR = 2
LRU_WIDTH = 1024
LRU_C = 8.0
CONV_WIDTH = 4
RWKV_WIDTH = 1024
HEAD_DIM = 64
DECAY_LORA = 64
AAA_LORA = 64
GATE_LORA = 160
PEER_HEADS = 8
PEER_N_KEYS = 128
PEER_TOPK = 16
LN_EPS = 1e-5
GN_EPS = 64e-5

LANES = 128
SUBLANES = 8
PAIR = LANES
CHUNK = 64
VMEM_LIMIT = 56 * 1024 * 1024


def _pick_block(n, target, align):
    best = None
    for b in range(align, min(n, target) + 1, align):
        if n % b == 0:
            best = b
    return best if best is not None else n


def _dot_nt(a, b):
    return lax.dot_general(a, b, (((1,), (1,)), ((), ())), preferred_element_type=F32)


def _dot_tn(a, b):
    return lax.dot_general(a, b, (((0,), (0,)), ((), ())), preferred_element_type=F32)


def _dot(a, b):
    return jnp.dot(a, b, preferred_element_type=F32)


def _dot_hi(a, b):
    return jnp.dot(a, b, precision=HI, preferred_element_type=F32)


def _split(x, n):
    out = []
    for _ in range(n - 1):
        p = x.astype(BF16)
        out.append(p)
        x = x - p.astype(F32)
    out.append(x.astype(BF16))
    return out


def _dot_exact_rhs(a, b, n):
    b = b.astype(BF16)
    acc = None
    for p in _split(a, n):
        d = _dot(p, b)
        acc = d if acc is None else acc + d
    return acc


def _dot_exact_lhs(a, b, n):
    a = a.astype(BF16)
    acc = None
    for p in _split(b, n):
        d = _dot(a, p)
        acc = d if acc is None else acc + d
    return acc


def _dot_3pass(a, b):
    ah, al = _split(a, 2)
    bh, bl = _split(b, 2)
    return _dot(ah, bh) + (_dot(ah, bl) + _dot(al, bh))


def _softplus(x):
    return jnp.maximum(x, 0.0) + jnp.log1p(jnp.exp(-jnp.abs(x)))


def _mm_body(x_ref, w_ref, o_ref):
    o_ref[...] = _dot(x_ref[...].astype(BF16), w_ref[...].astype(BF16)).astype(o_ref.dtype)


def matmul(x, w, bm_target=1088, bn_target=512):
    M, K = x.shape
    N = w.shape[1]
    bm = _pick_block(M, bm_target, 8)
    bn = _pick_block(N, bn_target, LANES)
    return pl.pallas_call(
        _mm_body,
        grid=(M // bm, N // bn),
        in_specs=[pl.BlockSpec((bm, K), lambda i, j: (i, 0)),
                  pl.BlockSpec((K, bn), lambda i, j: (0, j))],
        out_specs=pl.BlockSpec((bm, bn), lambda i, j: (i, j)),
        out_shape=jax.ShapeDtypeStruct((M, N), F32),
        compiler_params=pltpu.CompilerParams(
            dimension_semantics=("parallel", "parallel"), vmem_limit_bytes=VMEM_LIMIT),
        name="matmul",
    )(x, w)


def _prep_body(p_lru_ref, lo_lru_ref, hi_lru_ref, p_rw_ref, lo_rw_ref, hi_rw_ref,
               cw_ref, cb_ref, mu_ref, xc_ref, rw_ref, *, nb_ctx):
    t = pl.program_id(1)
    tb = p_lru_ref.shape[1]
    first = (t == 0) | (t == nb_ctx)
    last = (t == nb_ctx - 1) | (t == pl.num_programs(1) - 1)
    row = lax.broadcasted_iota(jnp.int32, (tb, 1), 0)

    def shifted(x, lo, hi, d):
        if d == 0:
            return x
        y = pltpu.roll(x, (-d) % tb, axis=0)
        if d < 0:
            for i in range(-d):
                fill = jnp.where(first, 0.0, lo[SUBLANES + d + i:SUBLANES + d + i + 1, :])
                y = jnp.where(row == i, fill, y)
        else:
            for i in range(d):
                fill = jnp.where(last, 0.0, hi[i:i + 1, :])
                y = jnp.where(row == tb - d + i, fill, y)
        return y

    x = p_lru_ref[0]
    lo, hi = lo_lru_ref[0], hi_lru_ref[0]
    acc = cb_ref[...] + jnp.zeros_like(x)
    for tap in range(CONV_WIDTH):
        acc = acc + shifted(x, lo, hi, tap - CONV_WIDTH // 2) * cw_ref[tap:tap + 1, :]
    xc_ref[0] = acc

    z = p_rw_ref[0]
    lo, hi = lo_rw_ref[0], hi_rw_ref[0]
    rw_ref[0] = z + mu_ref[0:1, :] * (shifted(z, lo, hi, -1) - z) + mu_ref[1:2, :] * (shifted(z, lo, hi, 1) - z)


def mixer_prep(p_lg, p_rw, conv_w, conv_b, mu_p, n_ctx):
    B, TT, RW = p_rw.shape
    C = LRU_WIDTH
    tb = _pick_block(math.gcd(n_ctx, TT - n_ctx), 256, SUBLANES)
    nb, nb_ctx, r8 = TT // tb, n_ctx // tb, tb // SUBLANES
    cur = lambda b, t: (b, t, 0)
    lo = lambda b, t: (b, jnp.maximum(t * r8 - 1, 0), 0)
    hi = lambda b, t: (b, jnp.minimum((t + 1) * r8, TT // SUBLANES - 1), 0)
    return pl.pallas_call(
        functools.partial(_prep_body, nb_ctx=nb_ctx),
        grid=(B, nb),
        in_specs=[pl.BlockSpec((1, tb, C), cur), pl.BlockSpec((1, SUBLANES, C), lo),
                  pl.BlockSpec((1, SUBLANES, C), hi),
                  pl.BlockSpec((1, tb, RW), cur), pl.BlockSpec((1, SUBLANES, RW), lo),
                  pl.BlockSpec((1, SUBLANES, RW), hi),
                  pl.BlockSpec(conv_w.shape, lambda b, t: (0, 0)),
                  pl.BlockSpec((1, C), lambda b, t: (0, 0)),
                  pl.BlockSpec(mu_p.shape, lambda b, t: (0, 0))],
        out_specs=[pl.BlockSpec((1, tb, C), cur), pl.BlockSpec((1, tb, RW), cur)],
        out_shape=[jax.ShapeDtypeStruct((B, TT, C), F32), jax.ShapeDtypeStruct((B, TT, RW), F32)],
        compiler_params=pltpu.CompilerParams(
            dimension_semantics=("parallel", "parallel"), vmem_limit_bytes=VMEM_LIMIT),
        name="mixer_prep",
    )(p_lg, p_lg, p_lg, p_rw, p_rw, p_rw, conv_w, conv_b[None], mu_p)


def _scan_block(g, t, n_batch, nb_ctx, nb):
    bwd = jnp.where(t < nb_ctx, nb_ctx - 1 - t, nb - 1 - (t - nb_ctx))
    return jnp.where(g // n_batch == 1, bwd, t)


def _lru_body(xc_ref, w_ref, ba_ref, bx_ref, lam_ref, y_ref, a_scr, b_scr, h_scr, *, n_batch):
    tb = xc_ref.shape[1]
    n_pair = xc_ref.shape[2] // PAIR
    rev = pl.program_id(0) // n_batch == 1

    @pl.when(pl.program_id(1) == 0)
    def _():
        h_scr[...] = jnp.zeros_like(h_scr)

    xc = xc_ref[0]
    ra, ia = [], []
    for p in range(n_pair):
        g = _dot(xc[:, p * PAIR:(p + 1) * PAIR].astype(BF16), w_ref[0, p])
        ra.append(g[:, :PAIR])
        ia.append(g[:, PAIR:])
    r = jax.nn.sigmoid(jnp.concatenate(ra, axis=1) + ba_ref[0])
    i = jax.nn.sigmoid(jnp.concatenate(ia, axis=1) + bx_ref[0])
    log_a = -LRU_C * r * _softplus(-lam_ref[0])
    a_scr[...] = jnp.exp(log_a)
    b_scr[...] = jnp.sqrt(jnp.maximum(1.0 - jnp.exp(2.0 * log_a), 0.0)) * (i * xc)

    def step(t, h):
        tt = jnp.where(rev, tb - 1 - t, t)
        h = a_scr[pl.ds(tt, 1), :] * h + b_scr[pl.ds(tt, 1), :]
        y_ref[0, pl.ds(tt, 1), :] = h
        return h

    h_scr[...] = lax.fori_loop(0, tb, step, h_scr[...], unroll=8)


def lru_scan(xc, w_bd, ba, bx, lam, n_ctx):
    B, TT, C = xc.shape
    tb = _pick_block(math.gcd(n_ctx, TT - n_ctx), 256, 8)
    nb, nb_ctx = TT // tb, n_ctx // tb
    seq = lambda g, t: (g % B, _scan_block(g, t, B, nb_ctx, nb), 0)
    dmap = lambda g, t: (g // B, 0, 0)
    return pl.pallas_call(
        functools.partial(_lru_body, n_batch=B),
        grid=(N_DIR * B, nb),
        in_specs=[pl.BlockSpec((1, tb, C), seq),
                  pl.BlockSpec((1, C // PAIR, PAIR, 2 * PAIR), lambda g, t: (g // B, 0, 0, 0)),
                  pl.BlockSpec((1, 1, C), dmap),
                  pl.BlockSpec((1, 1, C), dmap),
                  pl.BlockSpec((1, 1, C), dmap)],
        out_specs=pl.BlockSpec((1, tb, C), lambda g, t: (g, _scan_block(g, t, B, nb_ctx, nb), 0)),
        out_shape=jax.ShapeDtypeStruct((N_DIR * B, TT, C), F32),
        scratch_shapes=[pltpu.VMEM((tb, C), F32), pltpu.VMEM((tb, C), F32), pltpu.VMEM((1, C), F32)],
        compiler_params=pltpu.CompilerParams(
            dimension_semantics=("parallel", "arbitrary"), vmem_limit_bytes=VMEM_LIMIT),
        name="lru_scan",
    )(xc, w_bd, ba, bx, lam)


def _rwkv_body(r_ref, k_ref, v_ref, wd_ref, ad_ref, w0_ref, w2_ref, a0_ref, a2_ref,
               kk_ref, ka_ref, rk_ref, y_ref, bo_ref, s_ref, *, n_batch):
    L = r_ref.shape[1]
    C = r_ref.shape[2]
    n_pair = C // PAIR
    pairs = range(n_pair)
    sgn = jnp.where(pl.program_id(0) // n_batch == 1, -1, 1)

    @pl.when(pl.program_id(1) == 0)
    def _():
        s_ref[...] = jnp.zeros_like(s_ref)

    wpre = w0_ref[0] + _dot_3pass(jnp.tanh(wd_ref[0]), w2_ref[0])
    ld = -jnp.exp(-_softplus(-wpre) - 0.5)
    alr = jax.nn.sigmoid(a0_ref[0] + _dot_3pass(ad_ref[0], a2_ref[0]))

    ti = lax.broadcasted_iota(jnp.int32, (L, L), 0)
    tj = lax.broadcasted_iota(jnp.int32, (L, L), 1)
    cs = _dot_exact_lhs((((ti - tj) * sgn) >= 0).astype(F32), ld, 3)
    cs_end = jnp.where(sgn < 0, cs[0:1, :], cs[L - 1:L, :])

    ri = lax.broadcasted_iota(jnp.int32, (PAIR, PAIR), 0)
    ci = lax.broadcasted_iota(jnp.int32, (PAIR, PAIR), 1)
    same = (ri // HEAD_DIM) == (ci // HEAD_DIM)
    bd_ones = same.astype(F32)
    order = (ri - ci) * sgn
    strict = same & (order > 0)
    incl = same & (order >= 0)
    eye = (ri == ci).astype(F32)
    m0 = lax.broadcasted_iota(jnp.int32, (L, PAIR), 1) < HEAD_DIM

    def dup(z):
        return jnp.concatenate([z, z], axis=0)

    def slab(z):
        return jnp.where(m0, z[:L], z[L:])

    at, rt, v2, bh, kh, vv, lhs4, bt2, kt2, g_end = [], [], [], [], [], [], [], [], [], []
    for p in pairs:
        sl = slice(p * PAIR, (p + 1) * PAIR)
        r = r_ref[0, :, sl]
        k = k_ref[0, :, sl]
        v = v_ref[0, :, sl]
        ld_p, alr_p, cs_p, cs_l = ld[:, sl], alr[:, sl], cs[:, sl], cs_end[:, sl]
        kk0 = k * kk_ref[:, sl]
        ssq = _dot_exact_rhs(kk0 * kk0, bd_ones, 2)
        kk = kk0 * lax.rsqrt(jnp.maximum(ssq, 1e-24))
        kd = k * (1.0 + (alr_p - 1.0) * ka_ref[:, sl])
        b = kk * alr_p
        bo_ref[0, :, sl] = _dot_exact_rhs(r * kd * rk_ref[:, sl], bd_ones, 2) * v
        g_inv = jnp.exp(-cs_p)
        g_rel = jnp.exp(cs_l - cs_p)
        at_p = -kk * jnp.exp(cs_p - ld_p)
        rt_p = r * jnp.exp(cs_p)
        zero = jnp.zeros_like(at_p)
        lhs4.append(jnp.concatenate([jnp.where(m0, at_p, zero), jnp.where(m0, zero, at_p),
                                     jnp.where(m0, rt_p, zero), jnp.where(m0, zero, rt_p)], axis=0).astype(BF16))
        bt2.append(dup(b * g_inv).astype(BF16))
        kt2.append(dup(kd * g_inv).astype(BF16))
        at.append(at_p)
        rt.append(rt_p)
        vv.append(v)
        v2.append(dup(v).astype(BF16))
        bh.append((b * g_rel).astype(BF16))
        kh.append((kd * g_rel).astype(BF16))
        g_end.append(jnp.exp(cs_l))

    ab = [_dot_nt(lhs4[p], bt2[p]) for p in pairs]
    ak = [_dot_nt(lhs4[p], kt2[p]) for p in pairs]
    a_ab = [jnp.where(strict, ab[p][:2 * L], 0.0) for p in pairs]
    a_rb = [jnp.where(incl, ab[p][2 * L:], 0.0).astype(BF16) for p in pairs]
    a_ak = [jnp.where(strict, ak[p][:2 * L], 0.0).astype(BF16) for p in pairs]
    a_rk = [jnp.where(incl, ak[p][2 * L:], 0.0).astype(BF16) for p in pairs]
    av = [_dot(a_ak[p], v2[p]) for p in pairs]
    kv = [_dot(a_rk[p], v2[p]) for p in pairs]

    def off_diag(blk):
        rb = ((ri % (2 * blk)) >= blk).astype(jnp.int32)
        cb = ((ci % (2 * blk)) >= blk).astype(jnp.int32)
        return same & ((ri // (2 * blk)) == (ci // (2 * blk))) & (((rb - cb) * sgn) == 1)

    m1 = off_diag(1)
    tinv = [eye + jnp.where(m1, a_ab[p], 0.0) for p in pairs]
    blk = 2
    while blk < L:
        mb = off_diag(blk)
        tb = [tinv[p].astype(BF16) for p in pairs]
        t1 = [_dot(tb[p], jnp.where(mb, a_ab[p], 0.0).astype(BF16)).astype(BF16) for p in pairs]
        tinv = [tinv[p] + _dot(t1[p], tb[p]) for p in pairs]
        blk *= 2

    x = [_dot(tinv[p].astype(BF16), jnp.concatenate([dup(at[p]), av[p]], axis=1).astype(BF16)) for p in pairs]
    abar = [slab(x[p][:, :PAIR]) for p in pairs]
    uv = [slab(x[p][:, PAIR:]) for p in pairs]
    z = [_dot(a_rb[p], jnp.concatenate([dup(abar[p]), dup(uv[p])], axis=1).astype(BF16)) for p in pairs]
    rbar = [(rt[p] + slab(z[p][:, :PAIR])).astype(BF16) for p in pairs]
    yv = [slab(z[p][:, PAIR:] + kv[p]) for p in pairs]
    mp = [jnp.where(same, _dot_tn(abar[p].astype(BF16), bh[p]), 0.0).astype(BF16) for p in pairs]
    sv = [jnp.where(same, _dot_tn(jnp.concatenate([uv[p], vv[p]], axis=0).astype(BF16),
                                  jnp.concatenate([bh[p], kh[p]], axis=0)), 0.0) for p in pairs]
    for p in pairs:
        sl = slice(p * PAIR, (p + 1) * PAIR)
        s0 = s_ref[p]
        s0b = s0.astype(BF16)
        y_ref[0, :, sl] = _dot_nt(rbar[p], s0b) + yv[p]
        s_ref[p] = s0 * g_end[p] + _dot(s0b, mp[p]) + sv[p]


def rwkv_scan(rw, w0, w2p, a0, a2p, k_k, k_a, r_k, n_ctx):
    B, TT, _ = rw.shape
    C = RWKV_WIDTH
    L = CHUNK
    assert n_ctx % L == 0 and TT % L == 0 and 2 * L == PAIR and C % PAIR == 0
    nb, nb_ctx = TT // L, n_ctx // L
    dmap3 = lambda g, c: (g // B, 0, 0)
    cmap = lambda g, c: (0, 0)
    seq = lambda j: pl.BlockSpec((1, L, C), lambda g, c: (g % B, _scan_block(g, c, B, nb_ctx, nb), j))
    lora = lambda j: pl.BlockSpec((1, L, PAIR), lambda g, c: (g % B, _scan_block(g, c, B, nb_ctx, nb), j))
    out = pl.BlockSpec((1, L, C), lambda g, c: (g, _scan_block(g, c, B, nb_ctx, nb), 0))
    return pl.pallas_call(
        functools.partial(_rwkv_body, n_batch=B),
        grid=(N_DIR * B, nb),
        in_specs=[seq(0), seq(1), seq(2), lora(3 * C // PAIR), lora(3 * C // PAIR + 1),
                  pl.BlockSpec((1, 1, C), dmap3), pl.BlockSpec((1, PAIR, C), dmap3),
                  pl.BlockSpec((1, 1, C), dmap3), pl.BlockSpec((1, PAIR, C), dmap3),
                  pl.BlockSpec((1, C), cmap), pl.BlockSpec((1, C), cmap), pl.BlockSpec((1, C), cmap)],
        out_specs=[out, out],
        out_shape=[jax.ShapeDtypeStruct((N_DIR * B, TT, C), F32)] * 2,
        scratch_shapes=[pltpu.VMEM((C // PAIR, PAIR, PAIR), F32)],
        compiler_params=pltpu.CompilerParams(
            dimension_semantics=("parallel", "arbitrary"), vmem_limit_bytes=VMEM_LIMIT),
        name="rwkv_scan",
    )(rw, rw, rw, rw, rw, w0, w2p, a0, a2p, k_k, k_a, r_k)


def _merge_body(pg_ref, yf_ref, yb_ref, rf_ref, rb_ref, bf_ref, bb_ref, gd_ref,
                gng_ref, gnb_ref, g2_ref, wo_ref, o_ref):
    left = jax.nn.gelu(pg_ref[0]) * (yf_ref[0] + yb_ref[0])
    ys = rf_ref[0] + rb_ref[0]
    ri = lax.broadcasted_iota(jnp.int32, (PAIR, PAIR), 0)
    ci = lax.broadcasted_iota(jnp.int32, (PAIR, PAIR), 1)
    bd_ones = ((ri // HEAD_DIM) == (ci // HEAD_DIM)).astype(F32)
    yn = []
    for p in range(ys.shape[1] // PAIR):
        sl = slice(p * PAIR, (p + 1) * PAIR)
        yp = ys[:, sl]
        d = yp - _dot_exact_rhs(yp, bd_ones, 3) * (1.0 / HEAD_DIM)
        var = _dot_exact_rhs(d * d, bd_ones, 3) * (1.0 / HEAD_DIM)
        yn.append(d * lax.rsqrt(var + GN_EPS))
    yn = jnp.concatenate(yn, axis=1) * gng_ref[...] + gnb_ref[...]
    gate = _dot(jax.nn.sigmoid(gd_ref[0]).astype(BF16), g2_ref[...])
    right = (yn + (bf_ref[0] + bb_ref[0])) * gate
    merged = jnp.concatenate([left, right], axis=1).astype(BF16)
    o_ref[0] = _dot(merged, wo_ref[...])


def mixer_out(p_lg, y, yr, bo, rw, gn_g, gn_b, g2p, w_out):
    B, TT, _ = p_lg.shape
    C = LRU_WIDTH
    GW = g2p.shape[0]
    D = w_out.shape[1]
    bm = _pick_block(TT, 256, SUBLANES)
    fwd = pl.BlockSpec((1, bm, C), lambda b, t: (b, t, 0))
    bwd = pl.BlockSpec((1, bm, C), lambda b, t: (B + b, t, 0))
    const = lambda shape: pl.BlockSpec(shape, lambda b, t: (0, 0))
    return pl.pallas_call(
        _merge_body,
        grid=(B, TT // bm),
        in_specs=[pl.BlockSpec((1, bm, C), lambda b, t: (b, t, 1)),
                  fwd, bwd, fwd, bwd, fwd, bwd,
                  pl.BlockSpec((1, bm, GW), lambda b, t: (b, t, rw.shape[2] // GW - 1)),
                  const((1, C)), const((1, C)), const(g2p.shape), const(w_out.shape)],
        out_specs=pl.BlockSpec((1, bm, D), lambda b, t: (b, t, 0)),
        out_shape=jax.ShapeDtypeStruct((B, TT, D), F32),
        compiler_params=pltpu.CompilerParams(
            dimension_semantics=("parallel", "parallel"), vmem_limit_bytes=VMEM_LIMIT),
        name="mixer_out",
    )(p_lg, y, y, yr, yr, bo, bo, rw, gn_g[None], gn_b[None], g2p, w_out)


def _norm_mod_body(z_ref, o_ref, gate_ref, g_ref, b_ref, sh_ref, sc_ref, zo_ref, h_ref, *, alpha, transpose):
    z = alpha * z_ref[0] + gate_ref[0, 0] * o_ref[0]
    mu = jnp.mean(z, axis=-1, keepdims=True)
    d = z - mu
    var = jnp.mean(d * d, axis=-1, keepdims=True)
    zn = d * lax.rsqrt(var + LN_EPS) * g_ref[...] + b_ref[...]
    zo_ref[0] = zn
    h = zn * (1.0 + sc_ref[0, 0]) + sh_ref[0, 0]
    if transpose:
        h_ref[...] = h.T.astype(BF16)
    else:
        h_ref[0] = h.astype(BF16)


def norm_mod(z, o, gate, ln_g, ln_b, shift, scale, n_ctx, alpha, transpose):
    B, TT, D = z.shape
    tb = _pick_block(math.gcd(n_ctx, TT - n_ctx) if n_ctx else TT, 256, LANES)
    nb, nb_ctx = TT // tb, n_ctx // tb
    cur = pl.BlockSpec((1, tb, D), lambda b, t: (b, t, 0))
    seg = pl.BlockSpec((1, 1, 1, D), lambda b, t: (b, jnp.where(t >= nb_ctx, 1, 0), 0, 0))
    vec = pl.BlockSpec((1, D), lambda b, t: (0, 0))
    if transpose:
        h_spec = pl.BlockSpec((D, tb), lambda b, t: (0, b * nb + t))
        h_shape = jax.ShapeDtypeStruct((D, B * TT), BF16)
    else:
        h_spec, h_shape = cur, jax.ShapeDtypeStruct((B, TT, D), BF16)
    return pl.pallas_call(
        functools.partial(_norm_mod_body, alpha=alpha, transpose=transpose),
        grid=(B, nb),
        in_specs=[cur, cur, seg, vec, vec, seg, seg],
        out_specs=[cur, h_spec],
        out_shape=[jax.ShapeDtypeStruct((B, TT, D), F32), h_shape],
        compiler_params=pltpu.CompilerParams(
            dimension_semantics=("parallel", "parallel"), vmem_limit_bytes=VMEM_LIMIT),
        name="norm_mod",
    )(z, o, gate, ln_g[None], ln_b[None], shift, scale)


def _top_values(x, n):
    rows = lax.broadcasted_iota(jnp.int32, x.shape, 0).astype(F32)
    out = []
    for _ in range(n):
        m = jnp.max(x, axis=0, keepdims=True)
        out.append(m)
        first = jnp.min(jnp.where(x == m, rows, float(x.shape[0])), axis=0, keepdims=True)
        x = jnp.where(rows == first, -jnp.inf, x)
    return out


def _candidate_rows(t1, t2, n):
    k = len(t1)
    t1c = jnp.concatenate(t1, axis=0)
    t2c = jnp.concatenate(t2, axis=0)
    row = lax.broadcasted_iota(jnp.int32, (SUBLANES, t1c.shape[1]), 0)
    groups = []
    for j in range(k):
        cnt = min(k, n // (j + 1))
        if cnt <= 1:
            break
        for g in range(0, cnt, SUBLANES):
            piece = t1c[g:g + SUBLANES] + t2[j]
            groups.append(piece if cnt - g >= SUBLANES else jnp.where(row < cnt - g, piece, -jnp.inf))
    j0 = j
    for g in range(j0, k, SUBLANES):
        piece = t1[0] + t2c[g:g + SUBLANES]
        groups.append(piece if k - g >= SUBLANES else jnp.where(row < k - g, piece, -jnp.inf))
    return jnp.concatenate(groups, axis=0)


def _peer_score_body(ht_ref, wqt_ref, sk_ref, t1_ref, s2_ref, e1_ref, e2_ref):
    n_heads = sk_ref.shape[0]
    dq = sk_ref.shape[3]
    qt = _dot(wqt_ref[...], ht_ref[...])
    for h in range(n_heads):
        s = [_dot_hi(sk_ref[h, half], qt[(2 * h + half) * dq:(2 * h + half + 1) * dq, :])
             for half in range(2)]
        t1 = _top_values(s[0], PEER_TOPK)
        t2 = _top_values(s[1], PEER_TOPK)
        top = _top_values(_candidate_rows(t1, t2, PEER_TOPK + 1), PEER_TOPK + 1)
        zsum = jnp.exp(top[0] - top[0])
        for t in top[1:PEER_TOPK]:
            zsum = zsum + jnp.exp(t - top[0])
        tau = 0.5 * (top[PEER_TOPK - 1] + top[PEER_TOPK])
        t1_ref[h] = jnp.where(s[0] >= t1[-1], tau - s[0], jnp.inf)
        s2_ref[h] = jnp.where(s[1] >= t2[-1], s[1], -jnp.inf)
        e1_ref[h] = jnp.exp(s[0] - t1[0]) / zsum
        e2_ref[h] = jnp.exp(s[1] - t2[0])


def peer_scores(ht, wqt, subkeys):
    D, M = ht.shape
    nh, _, nk, dq = subkeys.shape
    tb = _pick_block(M, 256, LANES)
    big = pl.BlockSpec((nh, nk, tb), lambda i: (0, 0, i))
    shp = jax.ShapeDtypeStruct((nh, nk, M), F32)
    return pl.pallas_call(
        _peer_score_body,
        grid=(M // tb,),
        in_specs=[pl.BlockSpec((D, tb), lambda i: (0, i)),
                  pl.BlockSpec(wqt.shape, lambda i: (0, 0)),
                  pl.BlockSpec(subkeys.shape, lambda i: (0, 0, 0, 0))],
        out_specs=[big, big, big, big],
        out_shape=[shp, shp, shp, shp],
        compiler_params=pltpu.CompilerParams(
            dimension_semantics=("parallel",), vmem_limit_bytes=VMEM_LIMIT),
        name="peer_scores",
    )(ht, wqt, subkeys)


E2_TILE = 64


def _peer_expert_body(ht_ref, u_ref, vt_ref, t1_ref, s2_ref, e1_ref, e2_ref, o_ref, act_scr, x_scr, ot_scr):
    j = pl.program_id(1)
    n_heads, nk, tb = s2_ref.shape
    eb = u_ref.shape[0]
    per = eb // nk

    @pl.when(j == 0)
    def _():
        ot_scr[...] = jnp.zeros_like(ot_scr)

    act_scr[...] = _dot(u_ref[...], ht_ref[...])
    t1full = [[t1_ref[h, pl.ds(j * per + c, 1), :] for h in range(n_heads)] for c in range(per)]
    e1full = [[e1_ref[h, pl.ds(j * per + c, 1), :] for h in range(n_heads)] for c in range(per)]
    for tj in range(tb // LANES):
        tsl = slice(tj * LANES, (tj + 1) * LANES)
        t1rows = [[r[:, tsl] for r in rows] for rows in t1full]
        e1rows = [[r[:, tsl] for r in rows] for rows in e1full]
        for et in range(nk // E2_TILE):
            esl = slice(et * E2_TILE, (et + 1) * E2_TILE)
            accs = [jnp.zeros((E2_TILE, LANES), F32) for _ in range(per)]
            for h in range(n_heads):
                s2t = s2_ref[h, esl, tsl]
                e2t = e2_ref[h, esl, tsl]
                for c in range(per):
                    accs[c] = accs[c] + jnp.where(s2t >= t1rows[c][h], e2t, 0.0) * e1rows[c][h]
            for c in range(per):
                rsl = slice(c * nk + et * E2_TILE, c * nk + (et + 1) * E2_TILE)
                a = act_scr[rsl, tsl]
                gelu = 0.5 * a * (1.0 + lax.erf(a * (2.0 ** -0.5)))
                x_scr[rsl, tsl] = (accs[c] * gelu).astype(BF16)
    ot_scr[...] += _dot(vt_ref[...], x_scr[...])

    @pl.when(j == pl.num_programs(1) - 1)
    def _():
        o_ref[...] = ot_scr[...].T


def peer_experts(ht, u, vt, t1, s2, e1, e2):
    D, M = ht.shape
    nh, nk, _ = s2.shape
    E = u.shape[0]
    tb = _pick_block(M, 512, LANES)
    eb = _pick_block(E, 512, nk)
    big = pl.BlockSpec((nh, nk, tb), lambda i, j: (0, 0, i))
    return pl.pallas_call(
        _peer_expert_body,
        grid=(M // tb, E // eb),
        in_specs=[pl.BlockSpec((D, tb), lambda i, j: (0, i)),
                  pl.BlockSpec((eb, D), lambda i, j: (j, 0)),
                  pl.BlockSpec((D, eb), lambda i, j: (0, j)),
                  big, big, big, big],
        out_specs=pl.BlockSpec((tb, D), lambda i, j: (i, 0)),
        out_shape=jax.ShapeDtypeStruct((M, D), F32),
        scratch_shapes=[pltpu.VMEM((eb, tb), F32), pltpu.VMEM((eb, tb), BF16), pltpu.VMEM((D, tb), F32)],
        compiler_params=pltpu.CompilerParams(
            dimension_semantics=("parallel", "arbitrary"), vmem_limit_bytes=VMEM_LIMIT),
        name="peer_experts",
    )(ht, u, vt, t1, s2, e1, e2)


def _modulate(z, shift, scale):
    return z * (1.0 + scale) + shift


def _grid_pos_embed(rows, dim):
    t = jnp.arange(rows * GRID_W)
    row = (t // GRID_W).astype(F32)
    col = (t % GRID_W).astype(F32)
    quarter = dim // 4
    freq = 1.0 / (10000.0 ** (jnp.arange(quarter, dtype=F32) / quarter))

    def sincos(p):
        ang = p[:, None] * freq[None, :]
        return jnp.concatenate([jnp.sin(ang), jnp.cos(ang)], axis=-1)

    return jnp.concatenate([sincos(row), sincos(col)], axis=-1)


def _pair_block_diag(w):
    nh = w.shape[0]
    w = w.reshape(nh // 2, 2, HEAD_DIM, HEAD_DIM)
    z = jnp.zeros_like(w[:, 0])
    top = jnp.concatenate([w[:, 0], z], axis=2)
    bot = jnp.concatenate([z, w[:, 1]], axis=2)
    return jnp.concatenate([top, bot], axis=1)


def _token_mixer(h_all, n_ctx, keep_ctx, w_in, conv_w, conv_b, wa, ba, wx, bx, lam,
                 mu, w0, w2, a0, a2, g2, k_k, k_a, r_k, gn_g, gn_b, w_out):
    B, TT, D = h_all.shape
    C = LRU_WIDTH
    in_cols = w_in.shape[1]
    pad_cols = -in_cols % 512
    w_in_p = jnp.pad(w_in, ((0, 0), (0, pad_cols))).astype(BF16)
    hb = h_all.astype(BF16).reshape(B * TT, D)
    p_lg = matmul(hb, w_in_p[:, :2 * C]).reshape(B, TT, -1)
    p_rw = matmul(hb, w_in_p[:, 2 * C:]).reshape(B, TT, -1)
    mu_p = jnp.pad(mu, ((0, 0), (0, p_rw.shape[-1] - mu.shape[1])))
    xc, rw = mixer_prep(p_lg, p_rw, conv_w, conv_b, mu_p, n_ctx)

    w_bd = jnp.concatenate([_pair_block_diag(wa.reshape((-1,) + wa.shape[2:])),
                            _pair_block_diag(wx.reshape((-1,) + wx.shape[2:]))], axis=2)
    w_bd = w_bd.reshape(N_DIR, C // PAIR, PAIR, 2 * PAIR).astype(BF16)
    y = lru_scan(xc, w_bd, ba[:, None, :], bx[:, None, :], lam[:, None, :], n_ctx)

    zl = jnp.zeros_like(w2[0])
    w2p = jnp.stack([jnp.concatenate([w2[0], zl], axis=0), jnp.concatenate([zl, w2[1]], axis=0)])
    a2p = jnp.stack([jnp.concatenate([a2[0], zl], axis=0), jnp.concatenate([zl, a2[1]], axis=0)])
    yr, bo = rwkv_scan(rw, w0[:, None, :], w2p, a0[:, None, :], a2p,
                       k_k[None], k_a[None], r_k.reshape(1, -1), n_ctx)

    gd_cols = rw.shape[-1] - 3 * RWKV_WIDTH - N_DIR * (DECAY_LORA + AAA_LORA)
    g2p = jnp.pad(g2, ((0, gd_cols - g2.shape[0]), (0, 0))).astype(BF16)
    out = mixer_out(p_lg, y, yr, bo, rw, gn_g, gn_b, g2p, w_out.astype(BF16))
    return out if keep_ctx else out[:, n_ctx:]


def _peer(ht, wq, subkeys, u, v):
    t1, s2, e1, e2 = peer_scores(ht, wq.T.astype(BF16), subkeys)
    return peer_experts(ht, u.astype(BF16), v.T.astype(BF16), t1, s2, e1, e2)


def _seg_mods(mod, B, D):
    lat = mod[:B].reshape(B, 6, D)
    ctx = jnp.broadcast_to(mod[B].reshape(1, 6, D), (B, 6, D))
    both = jnp.stack([ctx, lat], axis=1)
    return [both[:, :, i, None, :] for i in range(6)]


def _forward(x, c, ctx, c_ctx, ada_w, ada_b, w_in, lru_conv_w, lru_conv_b, lru_wa, lru_ba,
             lru_wx, lru_bx, lru_lambda, rwkv_mu, rwkv_w0, rwkv_w2, rwkv_a0, rwkv_a2, rwkv_g2,
             rwkv_k_k, rwkv_k_a, rwkv_r_k, rwkv_gn_g, rwkv_gn_b, w_out, ln1_g, ln1_b,
             peer_wq, peer_subkeys, peer_u, peer_v, ln2_g, ln2_b):
    B, T, D = x.shape
    depth = ada_w.shape[0]
    alpha = (2.0 * depth) ** 0.25
    n_ctx = ctx.shape[1]
    x = x + _grid_pos_embed(T // GRID_W, D).astype(x.dtype)[None]
    cin = jnp.concatenate([jax.nn.silu(c), jax.nn.silu(c_ctx)[None],
                           jnp.zeros((8 - B - 1, D), F32)], axis=0)
    mods = [_seg_mods(matmul(cin, ada_w[l]) + ada_b[l], B, D) for l in range(depth)]
    z = jnp.concatenate([ctx, x], axis=1)
    sh1, sc1 = mods[0][0], mods[0][1]
    h = jnp.concatenate([_modulate(ctx, sh1[:, 0], sc1[:, 0]), _modulate(x, sh1[:, 1], sc1[:, 1])],
                        axis=1).astype(BF16)
    for l in range(depth):
        keep_ctx = l < depth - 1
        _, _, gt1, sh2, sc2, gt2 = mods[l]
        o = _token_mixer(h, n_ctx, keep_ctx, w_in[l], lru_conv_w[l], lru_conv_b[l], lru_wa[l],
                         lru_ba[l], lru_wx[l], lru_bx[l], lru_lambda[l], rwkv_mu[l], rwkv_w0[l],
                         rwkv_w2[l], rwkv_a0[l], rwkv_a2[l], rwkv_g2[l], rwkv_k_k[l], rwkv_k_a[l],
                         rwkv_r_k[l], rwkv_gn_g[l], rwkv_gn_b[l], w_out[l])
        if not keep_ctx:
            z, n_ctx = z[:, n_ctx:], 0
        z, ht = norm_mod(z, o, gt1, ln1_g[l], ln1_b[l], sh2, sc2, n_ctx, alpha, transpose=True)
        f = _peer(ht, peer_wq[l], peer_subkeys[l], peer_u[l], peer_v[l]).reshape(z.shape)
        nxt = mods[min(l + 1, depth - 1)]
        z, h = norm_mod(z, f, gt2, ln2_g[l], ln2_b[l], nxt[0], nxt[1], n_ctx, alpha, transpose=False)
    return z


def kernel(x, c, ctx, c_ctx, ada_w, ada_b, w_in, lru_conv_w, lru_conv_b, lru_wa, lru_ba, lru_wx, lru_bx, lru_lambda, rwkv_mu, rwkv_w0, rwkv_w2, rwkv_a0, rwkv_a2, rwkv_g2, rwkv_k_k, rwkv_k_a, rwkv_r_k, rwkv_gn_g, rwkv_gn_b, w_out, ln1_g, ln1_b, peer_wq, peer_subkeys, peer_u, peer_v, ln2_g, ln2_b):
    return _forward(x, c, ctx, c_ctx, ada_w, ada_b, w_in, lru_conv_w, lru_conv_b, lru_wa, lru_ba,
                    lru_wx, lru_bx, lru_lambda, rwkv_mu, rwkv_w0, rwkv_w2, rwkv_a0, rwkv_a2, rwkv_g2,
                    rwkv_k_k, rwkv_k_a, rwkv_r_k, rwkv_gn_g, rwkv_gn_b, w_out, ln1_g, ln1_b,
                    peer_wq, peer_subkeys, peer_u, peer_v, ln2_g, ln2_b)
```

```python
import functools
import math

import jax
import jax.numpy as jnp
from jax import lax
from jax.experimental import pallas as pl
from jax.experimental.pallas import tpu as pltpu

F32 = jnp.float32
BF16 = jnp.bfloat16
HI = lax.Precision.HIGHEST

GRID_W = 64
N_DIR = 2
LRU_WIDTH = 1024
LRU_C = 8.0
CONV_WIDTH = 4
RWKV_WIDTH = 1024
HEAD_DIM = 64
DECAY_LORA = 64
AAA_LORA = 64
GATE_LORA = 160
PEER_HEADS = 8
PEER_N_KEYS = 128
PEER_TOPK = 16
LN_EPS = 1e-5
GN_EPS = 64e-5

LANES = 128
SUBLANES = 8
PAIR = LANES
CHUNK = 64
VMEM_LIMIT = 56 * 1024 * 1024


def _pick_block(n, target, align):
    best = None
    for b in range(align, min(n, target) + 1, align):
        if n % b == 0:
            best = b
    return best if best is not None else n


def _dot_nt(a, b):
    return lax.dot_general(a, b, (((1,), (1,)), ((), ())), preferred_element_type=F32)


def _dot_tn(a, b):
    return lax.dot_general(a, b, (((0,), (0,)), ((), ())), preferred_element_type=F32)


def _dot(a, b):
    return jnp.dot(a, b, preferred_element_type=F32)


def _dot_hi(a, b):
    return jnp.dot(a, b, precision=HI, preferred_element_type=F32)


def _split(x, n):
    out = []
    for _ in range(n - 1):
        p = x.astype(BF16)
        out.append(p)
        x = x - p.astype(F32)
    out.append(x.astype(BF16))
    return out


def _dot_exact_rhs(a, b, n):
    b = b.astype(BF16)
    acc = None
    for p in _split(a, n):
        d = _dot(p, b)
        acc = d if acc is None else acc + d
    return acc


def _dot_exact_lhs(a, b, n):
    a = a.astype(BF16)
    acc = None
    for p in _split(b, n):
        d = _dot(a, p)
        acc = d if acc is None else acc + d
    return acc


def _dot_3pass(a, b):
    ah, al = _split(a, 2)
    bh, bl = _split(b, 2)
    return _dot(ah, bh) + (_dot(ah, bl) + _dot(al, bh))


def _softplus(x):
    return jnp.maximum(x, 0.0) + jnp.log1p(jnp.exp(-jnp.abs(x)))


def _mm_body(x_ref, w_ref, o_ref):
    o_ref[...] = _dot(x_ref[...].astype(BF16), w_ref[...].astype(BF16)).astype(o_ref.dtype)


def matmul(x, w, col0=0, n_cols=None, bm_target=1088, bn_target=512):
    M, K = x.shape
    N = w.shape[1] - col0 if n_cols is None else n_cols
    bm = _pick_block(M, bm_target, 8)
    bn = _pick_block(math.gcd(N, col0) if col0 else N, bn_target, LANES)
    j0 = col0 // bn
    return pl.pallas_call(
        _mm_body,
        grid=(M // bm, N // bn),
        in_specs=[pl.BlockSpec((bm, K), lambda i, j: (i, 0)),
                  pl.BlockSpec((K, bn), lambda i, j: (0, j0 + j))],
        out_specs=pl.BlockSpec((bm, bn), lambda i, j: (i, j)),
        out_shape=jax.ShapeDtypeStruct((M, N), F32),
        compiler_params=pltpu.CompilerParams(
            dimension_semantics=("parallel", "parallel"), vmem_limit_bytes=VMEM_LIMIT),
        name="matmul",
    )(x, w)


def _prep_body(p_lru_ref, lo_lru_ref, hi_lru_ref, p_rw_ref, lo_rw_ref, hi_rw_ref,
               cw_ref, cb_ref, mu_ref, xc_ref, rw_ref, *, nb_ctx):
    t = pl.program_id(1)
    tb = p_lru_ref.shape[1]
    first = (t == 0) | (t == nb_ctx)
    last = (t == nb_ctx - 1) | (t == pl.num_programs(1) - 1)
    row = lax.broadcasted_iota(jnp.int32, (tb, 1), 0)

    def shifted(x, lo, hi, d):
        if d == 0:
            return x
        y = pltpu.roll(x, (-d) % tb, axis=0)
        if d < 0:
            for i in range(-d):
                fill = jnp.where(first, 0.0, lo[SUBLANES + d + i:SUBLANES + d + i + 1, :])
                y = jnp.where(row == i, fill, y)
        else:
            for i in range(d):
                fill = jnp.where(last, 0.0, hi[i:i + 1, :])
                y = jnp.where(row == tb - d + i, fill, y)
        return y

    x = p_lru_ref[0]
    lo, hi = lo_lru_ref[0], hi_lru_ref[0]
    acc = cb_ref[...] + jnp.zeros_like(x)
    for tap in range(CONV_WIDTH):
        acc = acc + shifted(x, lo, hi, tap - CONV_WIDTH // 2) * cw_ref[tap:tap + 1, :]
    xc_ref[0] = acc

    z = p_rw_ref[0]
    lo, hi = lo_rw_ref[0], hi_rw_ref[0]
    rw_ref[0] = z + mu_ref[0:1, :] * (shifted(z, lo, hi, -1) - z) + mu_ref[1:2, :] * (shifted(z, lo, hi, 1) - z)


def mixer_prep(p_lg, p_rw, conv_w, conv_b, mu_p, n_ctx):
    B, TT, RW = p_rw.shape
    C = LRU_WIDTH
    tb = _pick_block(math.gcd(n_ctx, TT - n_ctx), 256, SUBLANES)
    nb, nb_ctx, r8 = TT // tb, n_ctx // tb, tb // SUBLANES
    cur = lambda b, t: (b, t, 0)
    lo = lambda b, t: (b, jnp.maximum(t * r8 - 1, 0), 0)
    hi = lambda b, t: (b, jnp.minimum((t + 1) * r8, TT // SUBLANES - 1), 0)
    return pl.pallas_call(
        functools.partial(_prep_body, nb_ctx=nb_ctx),
        grid=(B, nb),
        in_specs=[pl.BlockSpec((1, tb, C), cur), pl.BlockSpec((1, SUBLANES, C), lo),
                  pl.BlockSpec((1, SUBLANES, C), hi),
                  pl.BlockSpec((1, tb, RW), cur), pl.BlockSpec((1, SUBLANES, RW), lo),
                  pl.BlockSpec((1, SUBLANES, RW), hi),
                  pl.BlockSpec(conv_w.shape, lambda b, t: (0, 0)),
                  pl.BlockSpec((1, C), lambda b, t: (0, 0)),
                  pl.BlockSpec(mu_p.shape, lambda b, t: (0, 0))],
        out_specs=[pl.BlockSpec((1, tb, C), cur), pl.BlockSpec((1, tb, RW), cur)],
        out_shape=[jax.ShapeDtypeStruct((B, TT, C), F32), jax.ShapeDtypeStruct((B, TT, RW), F32)],
        compiler_params=pltpu.CompilerParams(
            dimension_semantics=("parallel", "parallel"), vmem_limit_bytes=VMEM_LIMIT),
        name="mixer_prep",
    )(p_lg, p_lg, p_lg, p_rw, p_rw, p_rw, conv_w, conv_b[None], mu_p)


def _scan_block(g, t, n_batch, nb_ctx, nb):
    bwd = jnp.where(t < nb_ctx, nb_ctx - 1 - t, nb - 1 - (t - nb_ctx))
    return jnp.where(g // n_batch == 1, bwd, t)


def _lru_body(xc_ref, w_ref, ba_ref, bx_ref, lam_ref, y_ref, a_scr, b_scr, h_scr, *, n_batch):
    tb = xc_ref.shape[1]
    n_pair = xc_ref.shape[2] // PAIR
    rev = pl.program_id(0) // n_batch == 1

    @pl.when(pl.program_id(1) == 0)
    def _():
        h_scr[...] = jnp.zeros_like(h_scr)

    xc = xc_ref[0]
    ra, ia = [], []
    for p in range(n_pair):
        g = _dot(xc[:, p * PAIR:(p + 1) * PAIR].astype(BF16), w_ref[0, p])
        ra.append(g[:, :PAIR])
        ia.append(g[:, PAIR:])
    r = jax.nn.sigmoid(jnp.concatenate(ra, axis=1) + ba_ref[0])
    i = jax.nn.sigmoid(jnp.concatenate(ia, axis=1) + bx_ref[0])
    log_a = -LRU_C * r * _softplus(-lam_ref[0])
    a_scr[...] = jnp.exp(log_a)
    b_scr[...] = jnp.sqrt(jnp.maximum(1.0 - jnp.exp(2.0 * log_a), 0.0)) * (i * xc)

    def step(t, h):
        tt = jnp.where(rev, tb - 1 - t, t)
        h = a_scr[pl.ds(tt, 1), :] * h + b_scr[pl.ds(tt, 1), :]
        y_ref[0, pl.ds(tt, 1), :] = h
        return h

    h_scr[...] = lax.fori_loop(0, tb, step, h_scr[...], unroll=8)


def lru_scan(xc, w_bd, ba, bx, lam, n_ctx):
    B, TT, C = xc.shape
    tb = _pick_block(math.gcd(n_ctx, TT - n_ctx), 256, 8)
    nb, nb_ctx = TT // tb, n_ctx // tb
    seq = lambda g, t: (g % B, _scan_block(g, t, B, nb_ctx, nb), 0)
    dmap = lambda g, t: (g // B, 0, 0)
    return pl.pallas_call(
        functools.partial(_lru_body, n_batch=B),
        grid=(N_DIR * B, nb),
        in_specs=[pl.BlockSpec((1, tb, C), seq),
                  pl.BlockSpec((1, C // PAIR, PAIR, 2 * PAIR), lambda g, t: (g // B, 0, 0, 0)),
                  pl.BlockSpec((1, 1, C), dmap),
                  pl.BlockSpec((1, 1, C), dmap),
                  pl.BlockSpec((1, 1, C), dmap)],
        out_specs=pl.BlockSpec((1, tb, C), lambda g, t: (g, _scan_block(g, t, B, nb_ctx, nb), 0)),
        out_shape=jax.ShapeDtypeStruct((N_DIR * B, TT, C), F32),
        scratch_shapes=[pltpu.VMEM((tb, C), F32), pltpu.VMEM((tb, C), F32), pltpu.VMEM((1, C), F32)],
        compiler_params=pltpu.CompilerParams(
            dimension_semantics=("parallel", "arbitrary"), vmem_limit_bytes=VMEM_LIMIT),
        name="lru_scan",
    )(xc, w_bd, ba, bx, lam)


def _rwkv_body(r_ref, k_ref, v_ref, wd_ref, ad_ref, w0_ref, w2_ref, a0_ref, a2_ref,
               kk_ref, ka_ref, rk_ref, y_ref, bo_ref, s_ref, *, n_batch):
    L = r_ref.shape[1]
    C = r_ref.shape[2]
    n_pair = C // PAIR
    pairs = range(n_pair)
    sgn = jnp.where(pl.program_id(0) // n_batch == 1, -1, 1)

    @pl.when(pl.program_id(1) == 0)
    def _():
        s_ref[...] = jnp.zeros_like(s_ref)

    wpre = w0_ref[0] + _dot_3pass(jnp.tanh(wd_ref[0]), w2_ref[0])
    ld = -jnp.exp(-_softplus(-wpre) - 0.5)
    alr = jax.nn.sigmoid(a0_ref[0] + _dot_3pass(ad_ref[0], a2_ref[0]))

    ti = lax.broadcasted_iota(jnp.int32, (L, L), 0)
    tj = lax.broadcasted_iota(jnp.int32, (L, L), 1)
    cs = _dot_exact_lhs((((ti - tj) * sgn) >= 0).astype(F32), ld, 3)
    cs_end = jnp.where(sgn < 0, cs[0:1, :], cs[L - 1:L, :])

    ri = lax.broadcasted_iota(jnp.int32, (PAIR, PAIR), 0)
    ci = lax.broadcasted_iota(jnp.int32, (PAIR, PAIR), 1)
    same = (ri // HEAD_DIM) == (ci // HEAD_DIM)
    bd_ones = same.astype(F32)
    order = (ri - ci) * sgn
    strict = same & (order > 0)
    incl = same & (order >= 0)
    eye = (ri == ci).astype(F32)
    m0 = lax.broadcasted_iota(jnp.int32, (L, PAIR), 1) < HEAD_DIM

    def dup(z):
        return jnp.concatenate([z, z], axis=0)

    def slab(z):
        return jnp.where(m0, z[:L], z[L:])

    at, rt, v2, bh, kh, vv, lhs4, bt2, kt2, g_end = [], [], [], [], [], [], [], [], [], []
    for p in pairs:
        sl = slice(p * PAIR, (p + 1) * PAIR)
        r = r_ref[0, :, sl]
        k = k_ref[0, :, sl]
        v = v_ref[0, :, sl]
        ld_p, alr_p, cs_p, cs_l = ld[:, sl], alr[:, sl], cs[:, sl], cs_end[:, sl]
        kk0 = k * kk_ref[:, sl]
        ssq = _dot_exact_rhs(kk0 * kk0, bd_ones, 2)
        kk = kk0 * lax.rsqrt(jnp.maximum(ssq, 1e-24))
        kd = k * (1.0 + (alr_p - 1.0) * ka_ref[:, sl])
        b = kk * alr_p
        bo_ref[0, :, sl] = _dot_exact_rhs(r * kd * rk_ref[:, sl], bd_ones, 2) * v
        g_inv = jnp.exp(-cs_p)
        g_rel = jnp.exp(cs_l - cs_p)
        at_p = -kk * jnp.exp(cs_p - ld_p)
        rt_p = r * jnp.exp(cs_p)
        zero = jnp.zeros_like(at_p)
        lhs4.append(jnp.concatenate([jnp.where(m0, at_p, zero), jnp.where(m0, zero, at_p),
                                     jnp.where(m0, rt_p, zero), jnp.where(m0, zero, rt_p)], axis=0).astype(BF16))
        bt2.append(dup(b * g_inv).astype(BF16))
        kt2.append(dup(kd * g_inv).astype(BF16))
        at.append(at_p)
        rt.append(rt_p)
        vv.append(v)
        v2.append(dup(v).astype(BF16))
        bh.append((b * g_rel).astype(BF16))
        kh.append((kd * g_rel).astype(BF16))
        g_end.append(jnp.exp(cs_l))

    ab = [_dot_nt(lhs4[p], bt2[p]) for p in pairs]
    ak = [_dot_nt(lhs4[p], kt2[p]) for p in pairs]
    a_ab = [jnp.where(strict, ab[p][:2 * L], 0.0) for p in pairs]
    a_rb = [jnp.where(incl, ab[p][2 * L:], 0.0).astype(BF16) for p in pairs]
    a_ak = [jnp.where(strict, ak[p][:2 * L], 0.0).astype(BF16) for p in pairs]
    a_rk = [jnp.where(incl, ak[p][2 * L:], 0.0).astype(BF16) for p in pairs]
    av = [_dot(a_ak[p], v2[p]) for p in pairs]
    kv = [_dot(a_rk[p], v2[p]) for p in pairs]

    def off_diag(blk):
        rb = ((ri % (2 * blk)) >= blk).astype(jnp.int32)
        cb = ((ci % (2 * blk)) >= blk).astype(jnp.int32)
        return same & ((ri // (2 * blk)) == (ci // (2 * blk))) & (((rb - cb) * sgn) == 1)

    m1 = off_diag(1)
    tinv = [eye + jnp.where(m1, a_ab[p], 0.0) for p in pairs]
    blk = 2
    while blk < L:
        mb = off_diag(blk)
        tb = [tinv[p].astype(BF16) for p in pairs]
        t1 = [_dot(tb[p], jnp.where(mb, a_ab[p], 0.0).astype(BF16)).astype(BF16) for p in pairs]
        tinv = [tinv[p] + _dot(t1[p], tb[p]) for p in pairs]
        blk *= 2

    x = [_dot(tinv[p].astype(BF16), jnp.concatenate([dup(at[p]), av[p]], axis=1).astype(BF16)) for p in pairs]
    abar = [slab(x[p][:, :PAIR]) for p in pairs]
    uv = [slab(x[p][:, PAIR:]) for p in pairs]
    z = [_dot(a_rb[p], jnp.concatenate([dup(abar[p]), dup(uv[p])], axis=1).astype(BF16)) for p in pairs]
    rbar = [(rt[p] + slab(z[p][:, :PAIR])).astype(BF16) for p in pairs]
    yv = [slab(z[p][:, PAIR:] + kv[p]) for p in pairs]
    mp = [jnp.where(same, _dot_tn(abar[p].astype(BF16), bh[p]), 0.0).astype(BF16) for p in pairs]
    sv = [jnp.where(same, _dot_tn(jnp.concatenate([uv[p], vv[p]], axis=0).astype(BF16),
                                  jnp.concatenate([bh[p], kh[p]], axis=0)), 0.0) for p in pairs]
    for p in pairs:
        sl = slice(p * PAIR, (p + 1) * PAIR)
        s0 = s_ref[p]
        s0b = s0.astype(BF16)
        y_ref[0, :, sl] = _dot_nt(rbar[p], s0b) + yv[p]
        s_ref[p] = s0 * g_end[p] + _dot(s0b, mp[p]) + sv[p]


def rwkv_scan(rw, w0, w2p, a0, a2p, k_k, k_a, r_k, n_ctx):
    B, TT, _ = rw.shape
    C = RWKV_WIDTH
    L = CHUNK
    assert n_ctx % L == 0 and TT % L == 0 and 2 * L == PAIR and C % PAIR == 0
    nb, nb_ctx = TT // L, n_ctx // L
    dmap3 = lambda g, c: (g // B, 0, 0)
    cmap = lambda g, c: (0, 0)
    seq = lambda j: pl.BlockSpec((1, L, C), lambda g, c: (g % B, _scan_block(g, c, B, nb_ctx, nb), j))
    lora = lambda j: pl.BlockSpec((1, L, PAIR), lambda g, c: (g % B, _scan_block(g, c, B, nb_ctx, nb), j))
    out = pl.BlockSpec((1, L, C), lambda g, c: (g, _scan_block(g, c, B, nb_ctx, nb), 0))
    return pl.pallas_call(
        functools.partial(_rwkv_body, n_batch=B),
        grid=(N_DIR * B, nb),
        in_specs=[seq(0), seq(1), seq(2), lora(3 * C // PAIR), lora(3 * C // PAIR + 1),
                  pl.BlockSpec((1, 1, C), dmap3), pl.BlockSpec((1, PAIR, C), dmap3),
                  pl.BlockSpec((1, 1, C), dmap3), pl.BlockSpec((1, PAIR, C), dmap3),
                  pl.BlockSpec((1, C), cmap), pl.BlockSpec((1, C), cmap), pl.BlockSpec((1, C), cmap)],
        out_specs=[out, out],
        out_shape=[jax.ShapeDtypeStruct((N_DIR * B, TT, C), F32)] * 2,
        scratch_shapes=[pltpu.VMEM((C // PAIR, PAIR, PAIR), F32)],
        compiler_params=pltpu.CompilerParams(
            dimension_semantics=("parallel", "arbitrary"), vmem_limit_bytes=VMEM_LIMIT),
        name="rwkv_scan",
    )(rw, rw, rw, rw, rw, w0, w2p, a0, a2p, k_k, k_a, r_k)


def _merge_body(pg_ref, yf_ref, yb_ref, rf_ref, rb_ref, bf_ref, bb_ref, gd_ref,
                gng_ref, gnb_ref, g2_ref, wo_ref, o_ref):
    left = jax.nn.gelu(pg_ref[0]) * (yf_ref[0] + yb_ref[0])
    ys = rf_ref[0] + rb_ref[0]
    ri = lax.broadcasted_iota(jnp.int32, (PAIR, PAIR), 0)
    ci = lax.broadcasted_iota(jnp.int32, (PAIR, PAIR), 1)
    bd_ones = ((ri // HEAD_DIM) == (ci // HEAD_DIM)).astype(F32)
    yn = []
    for p in range(ys.shape[1] // PAIR):
        sl = slice(p * PAIR, (p + 1) * PAIR)
        yp = ys[:, sl]
        d = yp - _dot_exact_rhs(yp, bd_ones, 3) * (1.0 / HEAD_DIM)
        var = _dot_exact_rhs(d * d, bd_ones, 3) * (1.0 / HEAD_DIM)
        yn.append(d * lax.rsqrt(var + GN_EPS))
    yn = jnp.concatenate(yn, axis=1) * gng_ref[...] + gnb_ref[...]
    gate = _dot(jax.nn.sigmoid(gd_ref[0]).astype(BF16), g2_ref[...])
    right = (yn + (bf_ref[0] + bb_ref[0])) * gate
    merged = jnp.concatenate([left, right], axis=1).astype(BF16)
    o_ref[0] = _dot(merged, wo_ref[...])


def mixer_out(p_lg, y, yr, bo, rw, gn_g, gn_b, g2p, w_out):
    B, TT, _ = p_lg.shape
    C = LRU_WIDTH
    GW = g2p.shape[0]
    D = w_out.shape[1]
    bm = _pick_block(TT, 256, SUBLANES)
    fwd = pl.BlockSpec((1, bm, C), lambda b, t: (b, t, 0))
    bwd = pl.BlockSpec((1, bm, C), lambda b, t: (B + b, t, 0))
    const = lambda shape: pl.BlockSpec(shape, lambda b, t: (0, 0))
    return pl.pallas_call(
        _merge_body,
        grid=(B, TT // bm),
        in_specs=[pl.BlockSpec((1, bm, C), lambda b, t: (b, t, 1)),
                  fwd, bwd, fwd, bwd, fwd, bwd,
                  pl.BlockSpec((1, bm, GW), lambda b, t: (b, t, rw.shape[2] // GW - 1)),
                  const((1, C)), const((1, C)), const(g2p.shape), const(w_out.shape)],
        out_specs=pl.BlockSpec((1, bm, D), lambda b, t: (b, t, 0)),
        out_shape=jax.ShapeDtypeStruct((B, TT, D), F32),
        compiler_params=pltpu.CompilerParams(
            dimension_semantics=("parallel", "parallel"), vmem_limit_bytes=VMEM_LIMIT),
        name="mixer_out",
    )(p_lg, y, y, yr, yr, bo, bo, rw, gn_g[None], gn_b[None], g2p, w_out)


def _norm_mod_body(z_ref, o_ref, gate_ref, g_ref, b_ref, sh_ref, sc_ref, zo_ref, h_ref, *, alpha, transpose):
    z = alpha * z_ref[0] + gate_ref[0, 0] * o_ref[0]
    mu = jnp.mean(z, axis=-1, keepdims=True)
    d = z - mu
    var = jnp.mean(d * d, axis=-1, keepdims=True)
    zn = d * lax.rsqrt(var + LN_EPS) * g_ref[...] + b_ref[...]
    zo_ref[0] = zn
    h = zn * (1.0 + sc_ref[0, 0]) + sh_ref[0, 0]
    if transpose:
        h_ref[...] = h.T.astype(BF16)
    else:
        h_ref[0] = h.astype(BF16)


def norm_mod(z, o, gate, ln_g, ln_b, shift, scale, n_ctx, alpha, transpose):
    B, TT, D = z.shape
    tb = _pick_block(math.gcd(n_ctx, TT - n_ctx) if n_ctx else TT, 256, LANES)
    nb, nb_ctx = TT // tb, n_ctx // tb
    cur = pl.BlockSpec((1, tb, D), lambda b, t: (b, t, 0))
    seg = pl.BlockSpec((1, 1, 1, D), lambda b, t: (b, jnp.where(t >= nb_ctx, 1, 0), 0, 0))
    vec = pl.BlockSpec((1, D), lambda b, t: (0, 0))
    if transpose:
        h_spec = pl.BlockSpec((D, tb), lambda b, t: (0, b * nb + t))
        h_shape = jax.ShapeDtypeStruct((D, B * TT), BF16)
    else:
        h_spec, h_shape = cur, jax.ShapeDtypeStruct((B, TT, D), BF16)
    return pl.pallas_call(
        functools.partial(_norm_mod_body, alpha=alpha, transpose=transpose),
        grid=(B, nb),
        in_specs=[cur, cur, seg, vec, vec, seg, seg],
        out_specs=[cur, h_spec],
        out_shape=[jax.ShapeDtypeStruct((B, TT, D), F32), h_shape],
        compiler_params=pltpu.CompilerParams(
            dimension_semantics=("parallel", "parallel"), vmem_limit_bytes=VMEM_LIMIT),
        name="norm_mod",
    )(z, o, gate, ln_g[None], ln_b[None], shift, scale)


def _top_values(x, n):
    rows = lax.broadcasted_iota(jnp.int32, x.shape, 0).astype(F32)
    out = []
    for _ in range(n):
        m = jnp.max(x, axis=0, keepdims=True)
        out.append(m)
        first = jnp.min(jnp.where(x == m, rows, float(x.shape[0])), axis=0, keepdims=True)
        x = jnp.where(rows == first, -jnp.inf, x)
    return out


def _candidate_rows(t1, t2, n):
    k = len(t1)
    t1c = jnp.concatenate(t1, axis=0)
    t2c = jnp.concatenate(t2, axis=0)
    row = lax.broadcasted_iota(jnp.int32, (SUBLANES, t1c.shape[1]), 0)
    groups = []
    for j in range(k):
        cnt = min(k, n // (j + 1))
        if cnt <= 1:
            break
        for g in range(0, cnt, SUBLANES):
            piece = t1c[g:g + SUBLANES] + t2[j]
            groups.append(piece if cnt - g >= SUBLANES else jnp.where(row < cnt - g, piece, -jnp.inf))
    j0 = j
    for g in range(j0, k, SUBLANES):
        piece = t1[0] + t2c[g:g + SUBLANES]
        groups.append(piece if k - g >= SUBLANES else jnp.where(row < k - g, piece, -jnp.inf))
    return jnp.concatenate(groups, axis=0)


def _peer_score_body(ht_ref, wqt_ref, sk_ref, t1_ref, s2_ref, e1_ref, e2_ref):
    n_heads = sk_ref.shape[0]
    dq = sk_ref.shape[3]
    qt = _dot(wqt_ref[...], ht_ref[...])
    for h in range(n_heads):
        s = [_dot_hi(sk_ref[h, half], qt[(2 * h + half) * dq:(2 * h + half + 1) * dq, :])
             for half in range(2)]
        t1 = _top_values(s[0], PEER_TOPK)
        t2 = _top_values(s[1], PEER_TOPK)
        top = _top_values(_candidate_rows(t1, t2, PEER_TOPK + 1), PEER_TOPK + 1)
        zsum = jnp.exp(top[0] - top[0])
        for t in top[1:PEER_TOPK]:
            zsum = zsum + jnp.exp(t - top[0])
        tau = 0.5 * (top[PEER_TOPK - 1] + top[PEER_TOPK])
        t1_ref[h] = jnp.where(s[0] >= t1[-1], tau - s[0], jnp.inf)
        s2_ref[h] = jnp.where(s[1] >= t2[-1], s[1], -jnp.inf)
        e1_ref[h] = jnp.exp(s[0] - t1[0]) / zsum
        e2_ref[h] = jnp.exp(s[1] - t2[0])


def peer_scores(ht, wqt, subkeys):
    D, M = ht.shape
    nh, _, nk, dq = subkeys.shape
    tb = _pick_block(M, 256, LANES)
    big = pl.BlockSpec((nh, nk, tb), lambda i: (0, 0, i))
    shp = jax.ShapeDtypeStruct((nh, nk, M), F32)
    return pl.pallas_call(
        _peer_score_body,
        grid=(M // tb,),
        in_specs=[pl.BlockSpec((D, tb), lambda i: (0, i)),
                  pl.BlockSpec(wqt.shape, lambda i: (0, 0)),
                  pl.BlockSpec(subkeys.shape, lambda i: (0, 0, 0, 0))],
        out_specs=[big, big, big, big],
        out_shape=[shp, shp, shp, shp],
        compiler_params=pltpu.CompilerParams(
            dimension_semantics=("parallel",), vmem_limit_bytes=VMEM_LIMIT),
        name="peer_scores",
    )(ht, wqt, subkeys)


def _tables_body(u_ref, v_ref, ub_ref, vt_ref):
    ub_ref[...] = u_ref[...].astype(BF16)
    vt_ref[...] = v_ref[...].T.astype(BF16)


def peer_tables(u, v):
    E, D = u.shape
    eb = _pick_block(E, 512, LANES)
    return pl.pallas_call(
        _tables_body,
        grid=(E // eb,),
        in_specs=[pl.BlockSpec((eb, D), lambda i: (i, 0)), pl.BlockSpec((eb, D), lambda i: (i, 0))],
        out_specs=[pl.BlockSpec((eb, D), lambda i: (i, 0)), pl.BlockSpec((D, eb), lambda i: (0, i))],
        out_shape=[jax.ShapeDtypeStruct((E, D), BF16), jax.ShapeDtypeStruct((D, E), BF16)],
        compiler_params=pltpu.CompilerParams(
            dimension_semantics=("parallel",), vmem_limit_bytes=VMEM_LIMIT),
        name="peer_tables",
    )(u, v)


E2_TILE = 64


def _peer_expert_body(ht_ref, u_ref, vt_ref, t1_ref, s2_ref, e1_ref, e2_ref, o_ref, act_scr, x_scr, ot_scr):
    j = pl.program_id(1)
    n_heads, nk, tb = s2_ref.shape
    eb = u_ref.shape[0]
    per = eb // nk

    @pl.when(j == 0)
    def _():
        ot_scr[...] = jnp.zeros_like(ot_scr)

    act_scr[...] = _dot(u_ref[...], ht_ref[...])
    t1full = [[t1_ref[h, pl.ds(j * per + c, 1), :] for h in range(n_heads)] for c in range(per)]
    e1full = [[e1_ref[h, pl.ds(j * per + c, 1), :] for h in range(n_heads)] for c in range(per)]
    for tj in range(tb // LANES):
        tsl = slice(tj * LANES, (tj + 1) * LANES)
        t1rows = [[r[:, tsl] for r in rows] for rows in t1full]
        e1rows = [[r[:, tsl] for r in rows] for rows in e1full]
        for et in range(nk // E2_TILE):
            esl = slice(et * E2_TILE, (et + 1) * E2_TILE)
            accs = [jnp.zeros((E2_TILE, LANES), F32) for _ in range(per)]
            for h in range(n_heads):
                s2t = s2_ref[h, esl, tsl]
                e2t = e2_ref[h, esl, tsl]
                for c in range(per):
                    accs[c] = accs[c] + jnp.where(s2t >= t1rows[c][h], e2t, 0.0) * e1rows[c][h]
            for c in range(per):
                rsl = slice(c * nk + et * E2_TILE, c * nk + (et + 1) * E2_TILE)
                a = act_scr[rsl, tsl]
                gelu = 0.5 * a * (1.0 + lax.erf(a * (2.0 ** -0.5)))
                x_scr[rsl, tsl] = (accs[c] * gelu).astype(BF16)
    ot_scr[...] += _dot(vt_ref[...], x_scr[...])

    @pl.when(j == pl.num_programs(1) - 1)
    def _():
        o_ref[...] = ot_scr[...].T


def peer_experts(ht, u, vt, t1, s2, e1, e2):
    D, M = ht.shape
    nh, nk, _ = s2.shape
    E = u.shape[0]
    tb = _pick_block(M, 512, LANES)
    eb = _pick_block(E, 1024, nk)
    big = pl.BlockSpec((nh, nk, tb), lambda i, j: (0, 0, i))
    return pl.pallas_call(
        _peer_expert_body,
        grid=(M // tb, E // eb),
        in_specs=[pl.BlockSpec((D, tb), lambda i, j: (0, i)),
                  pl.BlockSpec((eb, D), lambda i, j: (j, 0)),
                  pl.BlockSpec((D, eb), lambda i, j: (0, j)),
                  big, big, big, big],
        out_specs=pl.BlockSpec((tb, D), lambda i, j: (i, 0)),
        out_shape=jax.ShapeDtypeStruct((M, D), F32),
        scratch_shapes=[pltpu.VMEM((eb, tb), F32), pltpu.VMEM((eb, tb), BF16), pltpu.VMEM((D, tb), F32)],
        compiler_params=pltpu.CompilerParams(
            dimension_semantics=("parallel", "arbitrary"), vmem_limit_bytes=VMEM_LIMIT),
        name="peer_experts",
    )(ht, u, vt, t1, s2, e1, e2)


def _modulate(z, shift, scale):
    return z * (1.0 + scale) + shift


def _grid_pos_embed(rows, dim):
    t = jnp.arange(rows * GRID_W)
    row = (t // GRID_W).astype(F32)
    col = (t % GRID_W).astype(F32)
    quarter = dim // 4
    freq = 1.0 / (10000.0 ** (jnp.arange(quarter, dtype=F32) / quarter))

    def sincos(p):
        ang = p[:, None] * freq[None, :]
        return jnp.concatenate([jnp.sin(ang), jnp.cos(ang)], axis=-1)

    return jnp.concatenate([sincos(row), sincos(col)], axis=-1)


def _pair_block_diag(w):
    nh = w.shape[0]
    w = w.reshape(nh // 2, 2, HEAD_DIM, HEAD_DIM)
    z = jnp.zeros_like(w[:, 0])
    top = jnp.concatenate([w[:, 0], z], axis=2)
    bot = jnp.concatenate([z, w[:, 1]], axis=2)
    return jnp.concatenate([top, bot], axis=1)


def _token_mixer(h_all, n_ctx, keep_ctx, w_in, conv_w, conv_b, wa, ba, wx, bx, lam,
                 mu, w0, w2, a0, a2, g2, k_k, k_a, r_k, gn_g, gn_b, w_out):
    B, TT, D = h_all.shape
    C = LRU_WIDTH
    in_cols = w_in.shape[1]
    pad_cols = -in_cols % 512
    w_in_p = jnp.pad(w_in, ((0, 0), (0, pad_cols))).astype(BF16)
    hb = h_all.astype(BF16).reshape(B * TT, D)
    p_lg = matmul(hb, w_in_p, 0, 2 * C).reshape(B, TT, -1)
    p_rw = matmul(hb, w_in_p, 2 * C).reshape(B, TT, -1)
    mu_p = jnp.pad(mu, ((0, 0), (0, p_rw.shape[-1] - mu.shape[1])))
    xc, rw = mixer_prep(p_lg, p_rw, conv_w, conv_b, mu_p, n_ctx)

    w_bd = jnp.concatenate([_pair_block_diag(wa.reshape((-1,) + wa.shape[2:])),
                            _pair_block_diag(wx.reshape((-1,) + wx.shape[2:]))], axis=2)
    w_bd = w_bd.reshape(N_DIR, C // PAIR, PAIR, 2 * PAIR).astype(BF16)
    y = lru_scan(xc, w_bd, ba[:, None, :], bx[:, None, :], lam[:, None, :], n_ctx)

    zl = jnp.zeros_like(w2[0])
    w2p = jnp.stack([jnp.concatenate([w2[0], zl], axis=0), jnp.concatenate([zl, w2[1]], axis=0)])
    a2p = jnp.stack([jnp.concatenate([a2[0], zl], axis=0), jnp.concatenate([zl, a2[1]], axis=0)])
    yr, bo = rwkv_scan(rw, w0[:, None, :], w2p, a0[:, None, :], a2p,
                       k_k[None], k_a[None], r_k.reshape(1, -1), n_ctx)

    gd_cols = rw.shape[-1] - 3 * RWKV_WIDTH - N_DIR * (DECAY_LORA + AAA_LORA)
    g2p = jnp.pad(g2, ((0, gd_cols - g2.shape[0]), (0, 0))).astype(BF16)
    out = mixer_out(p_lg, y, yr, bo, rw, gn_g, gn_b, g2p, w_out.astype(BF16))
    return out if keep_ctx else out[:, n_ctx:]


def _peer(ht, wq, subkeys, u, v):
    t1, s2, e1, e2 = peer_scores(ht, wq.T.astype(BF16), subkeys)
    ub, vt = peer_tables(u, v)
    return peer_experts(ht, ub, vt, t1, s2, e1, e2)


def _seg_mods(mod, B, D):
    lat = mod[:B].reshape(B, 6, D)
    ctx = jnp.broadcast_to(mod[B].reshape(1, 6, D), (B, 6, D))
    both = jnp.stack([ctx, lat], axis=1)
    return [both[:, :, i, None, :] for i in range(6)]


def _forward(x, c, ctx, c_ctx, ada_w, ada_b, w_in, lru_conv_w, lru_conv_b, lru_wa, lru_ba,
             lru_wx, lru_bx, lru_lambda, rwkv_mu, rwkv_w0, rwkv_w2, rwkv_a0, rwkv_a2, rwkv_g2,
             rwkv_k_k, rwkv_k_a, rwkv_r_k, rwkv_gn_g, rwkv_gn_b, w_out, ln1_g, ln1_b,
             peer_wq, peer_subkeys, peer_u, peer_v, ln2_g, ln2_b):
    B, T, D = x.shape
    depth = ada_w.shape[0]
    alpha = (2.0 * depth) ** 0.25
    n_ctx = ctx.shape[1]
    x = x + _grid_pos_embed(T // GRID_W, D).astype(x.dtype)[None]
    cin = jnp.concatenate([jax.nn.silu(c), jax.nn.silu(c_ctx)[None],
                           jnp.zeros((8 - B - 1, D), F32)], axis=0)
    mods = [_seg_mods(matmul(cin, ada_w[l]) + ada_b[l], B, D) for l in range(depth)]
    z = jnp.concatenate([ctx, x], axis=1)
    sh1, sc1 = mods[0][0], mods[0][1]
    h = jnp.concatenate([_modulate(ctx, sh1[:, 0], sc1[:, 0]), _modulate(x, sh1[:, 1], sc1[:, 1])],
                        axis=1).astype(BF16)
    for l in range(depth):
        keep_ctx = l < depth - 1
        _, _, gt1, sh2, sc2, gt2 = mods[l]
        o = _token_mixer(h, n_ctx, keep_ctx, w_in[l], lru_conv_w[l], lru_conv_b[l], lru_wa[l],
                         lru_ba[l], lru_wx[l], lru_bx[l], lru_lambda[l], rwkv_mu[l], rwkv_w0[l],
                         rwkv_w2[l], rwkv_a0[l], rwkv_a2[l], rwkv_g2[l], rwkv_k_k[l], rwkv_k_a[l],
                         rwkv_r_k[l], rwkv_gn_g[l], rwkv_gn_b[l], w_out[l])
        if not keep_ctx:
            z, n_ctx = z[:, n_ctx:], 0
        z, ht = norm_mod(z, o, gt1, ln1_g[l], ln1_b[l], sh2, sc2, n_ctx, alpha, transpose=True)
        f = _peer(ht, peer_wq[l], peer_subkeys[l], peer_u[l], peer_v[l]).reshape(z.shape)
        nxt = mods[min(l + 1, depth - 1)]
        z, h = norm_mod(z, f, gt2, ln2_g[l], ln2_b[l], nxt[0], nxt[1], n_ctx, alpha, transpose=False)
    return z


def kernel(x, c, ctx, c_ctx, ada_w, ada_b, w_in, lru_conv_w, lru_conv_b, lru_wa, lru_ba, lru_wx, lru_bx, lru_lambda, rwkv_mu, rwkv_w0, rwkv_w2, rwkv_a0, rwkv_a2, rwkv_g2, rwkv_k_k, rwkv_k_a, rwkv_r_k, rwkv_gn_g, rwkv_gn_b, w_out, ln1_g, ln1_b, peer_wq, peer_subkeys, peer_u, peer_v, ln2_g, ln2_b):
    return _forward(x, c, ctx, c_ctx, ada_w, ada_b, w_in, lru_conv_w, lru_conv_b, lru_wa, lru_ba,
                    lru_wx, lru_bx, lru_lambda, rwkv_mu, rwkv_w0, rwkv_w2, rwkv_a0, rwkv_a2, rwkv_g2,
                    rwkv_k_k, rwkv_k_a, rwkv_r_k, rwkv_gn_g, rwkv_gn_b, w_out, ln1_g, ln1_b,
                    peer_wq, peer_subkeys, peer_u, peer_v, ln2_g, ln2_b)
```

```python
import functools
import math

import jax
import jax.numpy as jnp
from jax import lax
from jax.experimental import pallas as pl
from jax.experimental.pallas import tpu as pltpu

F32 = jnp.float32
BF16 = jnp.bfloat16
HI = lax.Precision.HIGHEST

GRID_W = 64
N_DIR = 2
LRU_WIDTH = 1024
LRU_C = 8.0
CONV_WIDTH = 4
RWKV_WIDTH = 1024
HEAD_DIM = 64
DECAY_LORA = 64
AAA_LORA = 64
GATE_LORA = 160
PEER_HEADS = 8
PEER_N_KEYS = 128
PEER_TOPK = 16
LN_EPS = 1e-5
GN_EPS = 64e-5

LANES = 128
SUBLANES = 8
PAIR = LANES
CHUNK = 64
VMEM_LIMIT = 56 * 1024 * 1024


def _pick_block(n, target, align):
    best = None
    for b in range(align, min(n, target) + 1, align):
        if n % b == 0:
            best = b
    return best if best is not None else n


def _dot_nt(a, b):
    return lax.dot_general(a, b, (((1,), (1,)), ((), ())), preferred_element_type=F32)


def _dot_tn(a, b):
    return lax.dot_general(a, b, (((0,), (0,)), ((), ())), preferred_element_type=F32)


def _dot(a, b):
    return jnp.dot(a, b, preferred_element_type=F32)


def _dot_hi(a, b):
    return jnp.dot(a, b, precision=HI, preferred_element_type=F32)


def _split(x, n):
    out = []
    for _ in range(n - 1):
        p = x.astype(BF16)
        out.append(p)
        x = x - p.astype(F32)
    out.append(x.astype(BF16))
    return out


def _dot_exact_rhs(a, b, n):
    b = b.astype(BF16)
    acc = None
    for p in _split(a, n):
        d = _dot(p, b)
        acc = d if acc is None else acc + d
    return acc


def _dot_exact_lhs(a, b, n):
    a = a.astype(BF16)
    acc = None
    for p in _split(b, n):
        d = _dot(a, p)
        acc = d if acc is None else acc + d
    return acc


def _dot_3pass(a, b):
    ah, al = _split(a, 2)
    bh, bl = _split(b, 2)
    return _dot(ah, bh) + (_dot(ah, bl) + _dot(al, bh))


def _softplus(x):
    return jnp.maximum(x, 0.0) + jnp.log1p(jnp.exp(-jnp.abs(x)))


def _mm_body(x_ref, w_ref, o_ref):
    o_ref[...] = _dot(x_ref[...].astype(BF16), w_ref[...].astype(BF16)).astype(o_ref.dtype)


def matmul(x, w, col0=0, n_cols=None, layer=None, bm_target=1088, bn_target=512):
    M, K = x.shape
    N = w.shape[-1] - col0 if n_cols is None else n_cols
    bm = _pick_block(M, bm_target, 8)
    bn = _pick_block(math.gcd(N, col0) if col0 else N, bn_target, LANES)
    j0 = col0 // bn
    if layer is None:
        w_spec = pl.BlockSpec((K, bn), lambda i, j: (0, j0 + j))
    else:
        w_spec = pl.BlockSpec((None, K, bn), lambda i, j: (layer, 0, j0 + j))
    return pl.pallas_call(
        _mm_body,
        grid=(M // bm, N // bn),
        in_specs=[pl.BlockSpec((bm, K), lambda i, j: (i, 0)), w_spec],
        out_specs=pl.BlockSpec((bm, bn), lambda i, j: (i, j)),
        out_shape=jax.ShapeDtypeStruct((M, N), F32),
        compiler_params=pltpu.CompilerParams(
            dimension_semantics=("parallel", "parallel"), vmem_limit_bytes=VMEM_LIMIT),
        name="matmul",
    )(x, w)


def _prep_body(p_lru_ref, lo_lru_ref, hi_lru_ref, p_rw_ref, lo_rw_ref, hi_rw_ref,
               cw_ref, cb_ref, mu_ref, xc_ref, rw_ref, *, nb_ctx):
    t = pl.program_id(1)
    tb = p_lru_ref.shape[1]
    first = (t == 0) | (t == nb_ctx)
    last = (t == nb_ctx - 1) | (t == pl.num_programs(1) - 1)
    row = lax.broadcasted_iota(jnp.int32, (tb, 1), 0)

    def shifted(x, lo, hi, d):
        if d == 0:
            return x
        y = pltpu.roll(x, (-d) % tb, axis=0)
        if d < 0:
            for i in range(-d):
                fill = jnp.where(first, 0.0, lo[SUBLANES + d + i:SUBLANES + d + i + 1, :])
                y = jnp.where(row == i, fill, y)
        else:
            for i in range(d):
                fill = jnp.where(last, 0.0, hi[i:i + 1, :])
                y = jnp.where(row == tb - d + i, fill, y)
        return y

    x = p_lru_ref[0]
    lo, hi = lo_lru_ref[0], hi_lru_ref[0]
    acc = cb_ref[...] + jnp.zeros_like(x)
    for tap in range(CONV_WIDTH):
        acc = acc + shifted(x, lo, hi, tap - CONV_WIDTH // 2) * cw_ref[tap:tap + 1, :]
    xc_ref[0] = acc

    z = p_rw_ref[0]
    lo, hi = lo_rw_ref[0], hi_rw_ref[0]
    rw_ref[0] = z + mu_ref[0:1, :] * (shifted(z, lo, hi, -1) - z) + mu_ref[1:2, :] * (shifted(z, lo, hi, 1) - z)


def mixer_prep(p_lg, p_rw, conv_w, conv_b, mu_p, n_ctx):
    B, TT, RW = p_rw.shape
    C = LRU_WIDTH
    tb = _pick_block(math.gcd(n_ctx, TT - n_ctx), 256, SUBLANES)
    nb, nb_ctx, r8 = TT // tb, n_ctx // tb, tb // SUBLANES
    cur = lambda b, t: (b, t, 0)
    lo = lambda b, t: (b, jnp.maximum(t * r8 - 1, 0), 0)
    hi = lambda b, t: (b, jnp.minimum((t + 1) * r8, TT // SUBLANES - 1), 0)
    return pl.pallas_call(
        functools.partial(_prep_body, nb_ctx=nb_ctx),
        grid=(B, nb),
        in_specs=[pl.BlockSpec((1, tb, C), cur), pl.BlockSpec((1, SUBLANES, C), lo),
                  pl.BlockSpec((1, SUBLANES, C), hi),
                  pl.BlockSpec((1, tb, RW), cur), pl.BlockSpec((1, SUBLANES, RW), lo),
                  pl.BlockSpec((1, SUBLANES, RW), hi),
                  pl.BlockSpec(conv_w.shape, lambda b, t: (0, 0)),
                  pl.BlockSpec((1, C), lambda b, t: (0, 0)),
                  pl.BlockSpec(mu_p.shape, lambda b, t: (0, 0))],
        out_specs=[pl.BlockSpec((1, tb, C), cur), pl.BlockSpec((1, tb, RW), cur)],
        out_shape=[jax.ShapeDtypeStruct((B, TT, C), F32), jax.ShapeDtypeStruct((B, TT, RW), F32)],
        compiler_params=pltpu.CompilerParams(
            dimension_semantics=("parallel", "parallel"), vmem_limit_bytes=VMEM_LIMIT),
        name="mixer_prep",
    )(p_lg, p_lg, p_lg, p_rw, p_rw, p_rw, conv_w, conv_b[None], mu_p)


def _scan_block(g, t, n_batch, nb_ctx, nb):
    bwd = jnp.where(t < nb_ctx, nb_ctx - 1 - t, nb - 1 - (t - nb_ctx))
    return jnp.where(g // n_batch == 1, bwd, t)


def _lru_body(xc_ref, w_ref, ba_ref, bx_ref, lam_ref, y_ref, a_scr, b_scr, h_scr, *, n_batch):
    tb = xc_ref.shape[1]
    n_pair = xc_ref.shape[2] // PAIR
    rev = pl.program_id(0) // n_batch == 1

    @pl.when(pl.program_id(1) == 0)
    def _():
        h_scr[...] = jnp.zeros_like(h_scr)

    xc = xc_ref[0]
    ra, ia = [], []
    for p in range(n_pair):
        g = _dot(xc[:, p * PAIR:(p + 1) * PAIR].astype(BF16), w_ref[0, p])
        ra.append(g[:, :PAIR])
        ia.append(g[:, PAIR:])
    r = jax.nn.sigmoid(jnp.concatenate(ra, axis=1) + ba_ref[0])
    i = jax.nn.sigmoid(jnp.concatenate(ia, axis=1) + bx_ref[0])
    log_a = -LRU_C * r * _softplus(-lam_ref[0])
    a_scr[...] = jnp.exp(log_a)
    b_scr[...] = jnp.sqrt(jnp.maximum(1.0 - jnp.exp(2.0 * log_a), 0.0)) * (i * xc)

    def step(t, h):
        tt = jnp.where(rev, tb - 1 - t, t)
        h = a_scr[pl.ds(tt, 1), :] * h + b_scr[pl.ds(tt, 1), :]
        y_ref[0, pl.ds(tt, 1), :] = h
        return h

    h_scr[...] = lax.fori_loop(0, tb, step, h_scr[...], unroll=8)


def lru_scan(xc, w_bd, ba, bx, lam, n_ctx):
    B, TT, C = xc.shape
    tb = _pick_block(math.gcd(n_ctx, TT - n_ctx), 256, 8)
    nb, nb_ctx = TT // tb, n_ctx // tb
    seq = lambda g, t: (g % B, _scan_block(g, t, B, nb_ctx, nb), 0)
    dmap = lambda g, t: (g // B, 0, 0)
    return pl.pallas_call(
        functools.partial(_lru_body, n_batch=B),
        grid=(N_DIR * B, nb),
        in_specs=[pl.BlockSpec((1, tb, C), seq),
                  pl.BlockSpec((1, C // PAIR, PAIR, 2 * PAIR), lambda g, t: (g // B, 0, 0, 0)),
                  pl.BlockSpec((1, 1, C), dmap),
                  pl.BlockSpec((1, 1, C), dmap),
                  pl.BlockSpec((1, 1, C), dmap)],
        out_specs=pl.BlockSpec((1, tb, C), lambda g, t: (g, _scan_block(g, t, B, nb_ctx, nb), 0)),
        out_shape=jax.ShapeDtypeStruct((N_DIR * B, TT, C), F32),
        scratch_shapes=[pltpu.VMEM((tb, C), F32), pltpu.VMEM((tb, C), F32), pltpu.VMEM((1, C), F32)],
        compiler_params=pltpu.CompilerParams(
            dimension_semantics=("parallel", "arbitrary"), vmem_limit_bytes=VMEM_LIMIT),
        name="lru_scan",
    )(xc, w_bd, ba, bx, lam)


def _rwkv_body(r_ref, k_ref, v_ref, wd_ref, ad_ref, w0_ref, w2_ref, a0_ref, a2_ref,
               kk_ref, ka_ref, rk_ref, y_ref, bo_ref, s_ref, *, n_batch):
    L = r_ref.shape[1]
    C = r_ref.shape[2]
    n_pair = C // PAIR
    pairs = range(n_pair)
    sgn = jnp.where(pl.program_id(0) // n_batch == 1, -1, 1)

    @pl.when(pl.program_id(1) == 0)
    def _():
        s_ref[...] = jnp.zeros_like(s_ref)

    wpre = w0_ref[0] + _dot_3pass(jnp.tanh(wd_ref[0]), w2_ref[0])
    ld = -jnp.exp(-_softplus(-wpre) - 0.5)
    alr = jax.nn.sigmoid(a0_ref[0] + _dot_3pass(ad_ref[0], a2_ref[0]))

    ti = lax.broadcasted_iota(jnp.int32, (L, L), 0)
    tj = lax.broadcasted_iota(jnp.int32, (L, L), 1)
    cs = _dot_exact_lhs((((ti - tj) * sgn) >= 0).astype(F32), ld, 3)
    cs_end = jnp.where(sgn < 0, cs[0:1, :], cs[L - 1:L, :])

    ri = lax.broadcasted_iota(jnp.int32, (PAIR, PAIR), 0)
    ci = lax.broadcasted_iota(jnp.int32, (PAIR, PAIR), 1)
    same = (ri // HEAD_DIM) == (ci // HEAD_DIM)
    bd_ones = same.astype(F32)
    order = (ri - ci) * sgn
    strict = same & (order > 0)
    incl = same & (order >= 0)
    eye = (ri == ci).astype(F32)
    m0 = lax.broadcasted_iota(jnp.int32, (L, PAIR), 1) < HEAD_DIM

    def dup(z):
        return jnp.concatenate([z, z], axis=0)

    def slab(z):
        return jnp.where(m0, z[:L], z[L:])

    at, rt, v2, bh, kh, vv, lhs4, bt2, kt2, g_end = [], [], [], [], [], [], [], [], [], []
    for p in pairs:
        sl = slice(p * PAIR, (p + 1) * PAIR)
        r = r_ref[0, :, sl]
        k = k_ref[0, :, sl]
        v = v_ref[0, :, sl]
        ld_p, alr_p, cs_p, cs_l = ld[:, sl], alr[:, sl], cs[:, sl], cs_end[:, sl]
        kk0 = k * kk_ref[:, sl]
        ssq = _dot_exact_rhs(kk0 * kk0, bd_ones, 2)
        kk = kk0 * lax.rsqrt(jnp.maximum(ssq, 1e-24))
        kd = k * (1.0 + (alr_p - 1.0) * ka_ref[:, sl])
        b = kk * alr_p
        bo_ref[0, :, sl] = _dot_exact_rhs(r * kd * rk_ref[:, sl], bd_ones, 2) * v
        g_inv = jnp.exp(-cs_p)
        g_rel = jnp.exp(cs_l - cs_p)
        at_p = -kk * jnp.exp(cs_p - ld_p)
        rt_p = r * jnp.exp(cs_p)
        zero = jnp.zeros_like(at_p)
        lhs4.append(jnp.concatenate([jnp.where(m0, at_p, zero), jnp.where(m0, zero, at_p),
                                     jnp.where(m0, rt_p, zero), jnp.where(m0, zero, rt_p)], axis=0).astype(BF16))
        bt2.append(dup(b * g_inv).astype(BF16))
        kt2.append(dup(kd * g_inv).astype(BF16))
        at.append(at_p)
        rt.append(rt_p)
        vv.append(v)
        v2.append(dup(v).astype(BF16))
        bh.append((b * g_rel).astype(BF16))
        kh.append((kd * g_rel).astype(BF16))
        g_end.append(jnp.exp(cs_l))

    ab = [_dot_nt(lhs4[p], bt2[p]) for p in pairs]
    ak = [_dot_nt(lhs4[p], kt2[p]) for p in pairs]
    a_ab = [jnp.where(strict, ab[p][:2 * L], 0.0) for p in pairs]
    a_rb = [jnp.where(incl, ab[p][2 * L:], 0.0).astype(BF16) for p in pairs]
    a_ak = [jnp.where(strict, ak[p][:2 * L], 0.0).astype(BF16) for p in pairs]
    a_rk = [jnp.where(incl, ak[p][2 * L:], 0.0).astype(BF16) for p in pairs]
    av = [_dot(a_ak[p], v2[p]) for p in pairs]
    kv = [_dot(a_rk[p], v2[p]) for p in pairs]

    def off_diag(blk):
        rb = ((ri % (2 * blk)) >= blk).astype(jnp.int32)
        cb = ((ci % (2 * blk)) >= blk).astype(jnp.int32)
        return same & ((ri // (2 * blk)) == (ci // (2 * blk))) & (((rb - cb) * sgn) == 1)

    m1 = off_diag(1)
    tinv = [eye + jnp.where(m1, a_ab[p], 0.0) for p in pairs]
    blk = 2
    while blk < L:
        mb = off_diag(blk)
        tb = [tinv[p].astype(BF16) for p in pairs]
        t1 = [_dot(tb[p], jnp.where(mb, a_ab[p], 0.0).astype(BF16)).astype(BF16) for p in pairs]
        tinv = [tinv[p] + _dot(t1[p], tb[p]) for p in pairs]
        blk *= 2

    x = [_dot(tinv[p].astype(BF16), jnp.concatenate([dup(at[p]), av[p]], axis=1).astype(BF16)) for p in pairs]
    abar = [slab(x[p][:, :PAIR]) for p in pairs]
    uv = [slab(x[p][:, PAIR:]) for p in pairs]
    z = [_dot(a_rb[p], jnp.concatenate([dup(abar[p]), dup(uv[p])], axis=1).astype(BF16)) for p in pairs]
    rbar = [(rt[p] + slab(z[p][:, :PAIR])).astype(BF16) for p in pairs]
    yv = [slab(z[p][:, PAIR:] + kv[p]) for p in pairs]
    mp = [jnp.where(same, _dot_tn(abar[p].astype(BF16), bh[p]), 0.0).astype(BF16) for p in pairs]
    sv = [jnp.where(same, _dot_tn(jnp.concatenate([uv[p], vv[p]], axis=0).astype(BF16),
                                  jnp.concatenate([bh[p], kh[p]], axis=0)), 0.0) for p in pairs]
    for p in pairs:
        sl = slice(p * PAIR, (p + 1) * PAIR)
        s0 = s_ref[p]
        s0b = s0.astype(BF16)
        y_ref[0, :, sl] = _dot_nt(rbar[p], s0b) + yv[p]
        s_ref[p] = s0 * g_end[p] + _dot(s0b, mp[p]) + sv[p]


def rwkv_scan(rw, w0, w2p, a0, a2p, k_k, k_a, r_k, n_ctx):
    B, TT, _ = rw.shape
    C = RWKV_WIDTH
    L = CHUNK
    assert n_ctx % L == 0 and TT % L == 0 and 2 * L == PAIR and C % PAIR == 0
    nb, nb_ctx = TT // L, n_ctx // L
    dmap3 = lambda g, c: (g // B, 0, 0)
    cmap = lambda g, c: (0, 0)
    seq = lambda j: pl.BlockSpec((1, L, C), lambda g, c: (g % B, _scan_block(g, c, B, nb_ctx, nb), j))
    lora = lambda j: pl.BlockSpec((1, L, PAIR), lambda g, c: (g % B, _scan_block(g, c, B, nb_ctx, nb), j))
    out = pl.BlockSpec((1, L, C), lambda g, c: (g, _scan_block(g, c, B, nb_ctx, nb), 0))
    return pl.pallas_call(
        functools.partial(_rwkv_body, n_batch=B),
        grid=(N_DIR * B, nb),
        in_specs=[seq(0), seq(1), seq(2), lora(3 * C // PAIR), lora(3 * C // PAIR + 1),
                  pl.BlockSpec((1, 1, C), dmap3), pl.BlockSpec((1, PAIR, C), dmap3),
                  pl.BlockSpec((1, 1, C), dmap3), pl.BlockSpec((1, PAIR, C), dmap3),
                  pl.BlockSpec((1, C), cmap), pl.BlockSpec((1, C), cmap), pl.BlockSpec((1, C), cmap)],
        out_specs=[out, out],
        out_shape=[jax.ShapeDtypeStruct((N_DIR * B, TT, C), F32)] * 2,
        scratch_shapes=[pltpu.VMEM((C // PAIR, PAIR, PAIR), F32)],
        compiler_params=pltpu.CompilerParams(
            dimension_semantics=("parallel", "arbitrary"), vmem_limit_bytes=VMEM_LIMIT),
        name="rwkv_scan",
    )(rw, rw, rw, rw, rw, w0, w2p, a0, a2p, k_k, k_a, r_k)


def _merge_body(pg_ref, yf_ref, yb_ref, rf_ref, rb_ref, bf_ref, bb_ref, gd_ref,
                gng_ref, gnb_ref, g2_ref, wo_ref, o_ref):
    left = jax.nn.gelu(pg_ref[0]) * (yf_ref[0] + yb_ref[0])
    ys = rf_ref[0] + rb_ref[0]
    ri = lax.broadcasted_iota(jnp.int32, (PAIR, PAIR), 0)
    ci = lax.broadcasted_iota(jnp.int32, (PAIR, PAIR), 1)
    bd_ones = ((ri // HEAD_DIM) == (ci // HEAD_DIM)).astype(F32)
    yn = []
    for p in range(ys.shape[1] // PAIR):
        sl = slice(p * PAIR, (p + 1) * PAIR)
        yp = ys[:, sl]
        d = yp - _dot_exact_rhs(yp, bd_ones, 3) * (1.0 / HEAD_DIM)
        var = _dot_exact_rhs(d * d, bd_ones, 3) * (1.0 / HEAD_DIM)
        yn.append(d * lax.rsqrt(var + GN_EPS))
    yn = jnp.concatenate(yn, axis=1) * gng_ref[...] + gnb_ref[...]
    gate = _dot(jax.nn.sigmoid(gd_ref[0]).astype(BF16), g2_ref[...])
    right = (yn + (bf_ref[0] + bb_ref[0])) * gate
    merged = jnp.concatenate([left, right], axis=1).astype(BF16)
    o_ref[0] = _dot(merged, wo_ref[...])


def mixer_out(p_lg, y, yr, bo, rw, gn_g, gn_b, g2p, w_out):
    B, TT, _ = p_lg.shape
    C = LRU_WIDTH
    GW = g2p.shape[0]
    D = w_out.shape[1]
    bm = _pick_block(TT, 256, SUBLANES)
    fwd = pl.BlockSpec((1, bm, C), lambda b, t: (b, t, 0))
    bwd = pl.BlockSpec((1, bm, C), lambda b, t: (B + b, t, 0))
    const = lambda shape: pl.BlockSpec(shape, lambda b, t: (0, 0))
    return pl.pallas_call(
        _merge_body,
        grid=(B, TT // bm),
        in_specs=[pl.BlockSpec((1, bm, C), lambda b, t: (b, t, 1)),
                  fwd, bwd, fwd, bwd, fwd, bwd,
                  pl.BlockSpec((1, bm, GW), lambda b, t: (b, t, rw.shape[2] // GW - 1)),
                  const((1, C)), const((1, C)), const(g2p.shape), const(w_out.shape)],
        out_specs=pl.BlockSpec((1, bm, D), lambda b, t: (b, t, 0)),
        out_shape=jax.ShapeDtypeStruct((B, TT, D), F32),
        compiler_params=pltpu.CompilerParams(
            dimension_semantics=("parallel", "parallel"), vmem_limit_bytes=VMEM_LIMIT),
        name="mixer_out",
    )(p_lg, y, y, yr, yr, bo, bo, rw, gn_g[None], gn_b[None], g2p, w_out)


def _norm_mod_body(z_ref, o_ref, gate_ref, g_ref, b_ref, sh_ref, sc_ref, zo_ref, h_ref, *, alpha, transpose):
    z = alpha * z_ref[0] + gate_ref[0, 0] * o_ref[0]
    mu = jnp.mean(z, axis=-1, keepdims=True)
    d = z - mu
    var = jnp.mean(d * d, axis=-1, keepdims=True)
    zn = d * lax.rsqrt(var + LN_EPS) * g_ref[...] + b_ref[...]
    zo_ref[0] = zn
    h = zn * (1.0 + sc_ref[0, 0]) + sh_ref[0, 0]
    if transpose:
        h_ref[...] = h.T.astype(BF16)
    else:
        h_ref[0] = h.astype(BF16)


def norm_mod(z, o, gate, ln_g, ln_b, shift, scale, n_ctx, alpha, transpose):
    B, TT, D = z.shape
    tb = _pick_block(math.gcd(n_ctx, TT - n_ctx) if n_ctx else TT, 256, LANES)
    nb, nb_ctx = TT // tb, n_ctx // tb
    cur = pl.BlockSpec((1, tb, D), lambda b, t: (b, t, 0))
    seg = pl.BlockSpec((1, 1, 1, D), lambda b, t: (b, jnp.where(t >= nb_ctx, 1, 0), 0, 0))
    vec = pl.BlockSpec((1, D), lambda b, t: (0, 0))
    if transpose:
        h_spec = pl.BlockSpec((D, tb), lambda b, t: (0, b * nb + t))
        h_shape = jax.ShapeDtypeStruct((D, B * TT), BF16)
    else:
        h_spec, h_shape = cur, jax.ShapeDtypeStruct((B, TT, D), BF16)
    return pl.pallas_call(
        functools.partial(_norm_mod_body, alpha=alpha, transpose=transpose),
        grid=(B, nb),
        in_specs=[cur, cur, seg, vec, vec, seg, seg],
        out_specs=[cur, h_spec],
        out_shape=[jax.ShapeDtypeStruct((B, TT, D), F32), h_shape],
        compiler_params=pltpu.CompilerParams(
            dimension_semantics=("parallel", "parallel"), vmem_limit_bytes=VMEM_LIMIT),
        name="norm_mod",
    )(z, o, gate, ln_g[None], ln_b[None], shift, scale)


def _top_values(x, n):
    rows = lax.broadcasted_iota(jnp.int32, x.shape, 0).astype(F32)
    out = []
    for _ in range(n):
        m = jnp.max(x, axis=0, keepdims=True)
        out.append(m)
        first = jnp.min(jnp.where(x == m, rows, float(x.shape[0])), axis=0, keepdims=True)
        x = jnp.where(rows == first, -jnp.inf, x)
    return out


def _candidate_rows(t1, t2, n):
    k = len(t1)
    t1c = jnp.concatenate(t1, axis=0)
    t2c = jnp.concatenate(t2, axis=0)
    row = lax.broadcasted_iota(jnp.int32, (SUBLANES, t1c.shape[1]), 0)
    groups = []
    for j in range(k):
        cnt = min(k, n // (j + 1))
        if cnt <= 1:
            break
        for g in range(0, cnt, SUBLANES):
            piece = t1c[g:g + SUBLANES] + t2[j]
            groups.append(piece if cnt - g >= SUBLANES else jnp.where(row < cnt - g, piece, -jnp.inf))
    j0 = j
    for g in range(j0, k, SUBLANES):
        piece = t1[0] + t2c[g:g + SUBLANES]
        groups.append(piece if k - g >= SUBLANES else jnp.where(row < k - g, piece, -jnp.inf))
    return jnp.concatenate(groups, axis=0)


def _peer_score_body(ht_ref, wqt_ref, sk_ref, t1_ref, s2_ref, e1_ref, e2_ref):
    n_heads = sk_ref.shape[0]
    dq = sk_ref.shape[3]
    qt = _dot(wqt_ref[...], ht_ref[...])
    for h in range(n_heads):
        s = [_dot_hi(sk_ref[h, half], qt[(2 * h + half) * dq:(2 * h + half + 1) * dq, :])
             for half in range(2)]
        t1 = _top_values(s[0], PEER_TOPK)
        t2 = _top_values(s[1], PEER_TOPK)
        top = _top_values(_candidate_rows(t1, t2, PEER_TOPK + 1), PEER_TOPK + 1)
        zsum = jnp.exp(top[0] - top[0])
        for t in top[1:PEER_TOPK]:
            zsum = zsum + jnp.exp(t - top[0])
        tau = 0.5 * (top[PEER_TOPK - 1] + top[PEER_TOPK])
        t1_ref[h] = jnp.where(s[0] >= t1[-1], tau - s[0], jnp.inf)
        s2_ref[h] = jnp.where(s[1] >= t2[-1], s[1], -jnp.inf)
        e1_ref[h] = jnp.exp(s[0] - t1[0]) / zsum
        e2_ref[h] = jnp.exp(s[1] - t2[0])


def peer_scores(ht, wqt, subkeys):
    D, M = ht.shape
    nh, _, nk, dq = subkeys.shape
    tb = _pick_block(M, 256, LANES)
    big = pl.BlockSpec((nh, nk, tb), lambda i: (0, 0, i))
    shp = jax.ShapeDtypeStruct((nh, nk, M), F32)
    return pl.pallas_call(
        _peer_score_body,
        grid=(M // tb,),
        in_specs=[pl.BlockSpec((D, tb), lambda i: (0, i)),
                  pl.BlockSpec(wqt.shape, lambda i: (0, 0)),
                  pl.BlockSpec(subkeys.shape, lambda i: (0, 0, 0, 0))],
        out_specs=[big, big, big, big],
        out_shape=[shp, shp, shp, shp],
        compiler_params=pltpu.CompilerParams(
            dimension_semantics=("parallel",), vmem_limit_bytes=VMEM_LIMIT),
        name="peer_scores",
    )(ht, wqt, subkeys)


def _tables_body(u_ref, v_ref, ub_ref, vt_ref):
    ub_ref[...] = u_ref[...].astype(BF16)
    vt_ref[...] = v_ref[...].T.astype(BF16)


def peer_tables(u, v, layer):
    _, E, D = u.shape
    eb = _pick_block(E, 512, LANES)
    src = pl.BlockSpec((None, eb, D), lambda i: (layer, i, 0))
    return pl.pallas_call(
        _tables_body,
        grid=(E // eb,),
        in_specs=[src, src],
        out_specs=[pl.BlockSpec((eb, D), lambda i: (i, 0)), pl.BlockSpec((D, eb), lambda i: (0, i))],
        out_shape=[jax.ShapeDtypeStruct((E, D), BF16), jax.ShapeDtypeStruct((D, E), BF16)],
        compiler_params=pltpu.CompilerParams(
            dimension_semantics=("parallel",), vmem_limit_bytes=VMEM_LIMIT),
        name="peer_tables",
    )(u, v)


E2_TILE = 64


def _peer_expert_body(ht_ref, u_ref, vt_ref, t1_ref, s2_ref, e1_ref, e2_ref, o_ref, act_scr, x_scr, ot_scr):
    j = pl.program_id(1)
    n_heads, nk, tb = s2_ref.shape
    eb = u_ref.shape[0]
    per = eb // nk

    @pl.when(j == 0)
    def _():
        ot_scr[...] = jnp.zeros_like(ot_scr)

    act_scr[...] = _dot(u_ref[...], ht_ref[...])
    t1full = [[t1_ref[h, pl.ds(j * per + c, 1), :] for h in range(n_heads)] for c in range(per)]
    e1full = [[e1_ref[h, pl.ds(j * per + c, 1), :] for h in range(n_heads)] for c in range(per)]
    for tj in range(tb // LANES):
        tsl = slice(tj * LANES, (tj + 1) * LANES)
        t1rows = [[r[:, tsl] for r in rows] for rows in t1full]
        e1rows = [[r[:, tsl] for r in rows] for rows in e1full]
        for et in range(nk // E2_TILE):
            esl = slice(et * E2_TILE, (et + 1) * E2_TILE)
            accs = [jnp.zeros((E2_TILE, LANES), F32) for _ in range(per)]
            for h in range(n_heads):
                s2t = s2_ref[h, esl, tsl]
                e2t = e2_ref[h, esl, tsl]
                for c in range(per):
                    accs[c] = accs[c] + jnp.where(s2t >= t1rows[c][h], e2t, 0.0) * e1rows[c][h]
            for c in range(per):
                rsl = slice(c * nk + et * E2_TILE, c * nk + (et + 1) * E2_TILE)
                a = act_scr[rsl, tsl]
                gelu = 0.5 * a * (1.0 + lax.erf(a * (2.0 ** -0.5)))
                x_scr[rsl, tsl] = (accs[c] * gelu).astype(BF16)
    ot_scr[...] += _dot(vt_ref[...], x_scr[...])

    @pl.when(j == pl.num_programs(1) - 1)
    def _():
        o_ref[...] = ot_scr[...].T


def peer_experts(ht, u, vt, t1, s2, e1, e2):
    D, M = ht.shape
    nh, nk, _ = s2.shape
    E = u.shape[0]
    tb = _pick_block(M, 512, LANES)
    eb = _pick_block(E, 1024, nk)
    big = pl.BlockSpec((nh, nk, tb), lambda i, j: (0, 0, i))
    return pl.pallas_call(
        _peer_expert_body,
        grid=(M // tb, E // eb),
        in_specs=[pl.BlockSpec((D, tb), lambda i, j: (0, i)),
                  pl.BlockSpec((eb, D), lambda i, j: (j, 0)),
                  pl.BlockSpec((D, eb), lambda i, j: (0, j)),
                  big, big, big, big],
        out_specs=pl.BlockSpec((tb, D), lambda i, j: (i, 0)),
        out_shape=jax.ShapeDtypeStruct((M, D), F32),
        scratch_shapes=[pltpu.VMEM((eb, tb), F32), pltpu.VMEM((eb, tb), BF16), pltpu.VMEM((D, tb), F32)],
        compiler_params=pltpu.CompilerParams(
            dimension_semantics=("parallel", "arbitrary"), vmem_limit_bytes=VMEM_LIMIT),
        name="peer_experts",
    )(ht, u, vt, t1, s2, e1, e2)


def _modulate(z, shift, scale):
    return z * (1.0 + scale) + shift


def _grid_pos_embed(rows, dim):
    t = jnp.arange(rows * GRID_W)
    row = (t // GRID_W).astype(F32)
    col = (t % GRID_W).astype(F32)
    quarter = dim // 4
    freq = 1.0 / (10000.0 ** (jnp.arange(quarter, dtype=F32) / quarter))

    def sincos(p):
        ang = p[:, None] * freq[None, :]
        return jnp.concatenate([jnp.sin(ang), jnp.cos(ang)], axis=-1)

    return jnp.concatenate([sincos(row), sincos(col)], axis=-1)


def _pair_block_diag(w):
    nh = w.shape[0]
    w = w.reshape(nh // 2, 2, HEAD_DIM, HEAD_DIM)
    z = jnp.zeros_like(w[:, 0])
    top = jnp.concatenate([w[:, 0], z], axis=2)
    bot = jnp.concatenate([z, w[:, 1]], axis=2)
    return jnp.concatenate([top, bot], axis=1)


def _token_mixer(h_all, n_ctx, keep_ctx, w_in, conv_w, conv_b, wa, ba, wx, bx, lam,
                 mu, w0, w2, a0, a2, g2, k_k, k_a, r_k, gn_g, gn_b, w_out):
    B, TT, D = h_all.shape
    C = LRU_WIDTH
    in_cols = w_in.shape[1]
    pad_cols = -in_cols % 512
    w_in_p = jnp.pad(w_in, ((0, 0), (0, pad_cols))).astype(BF16)
    hb = h_all.astype(BF16).reshape(B * TT, D)
    p_lg = matmul(hb, w_in_p, 0, 2 * C).reshape(B, TT, -1)
    p_rw = matmul(hb, w_in_p, 2 * C).reshape(B, TT, -1)
    mu_p = jnp.pad(mu, ((0, 0), (0, p_rw.shape[-1] - mu.shape[1])))
    xc, rw = mixer_prep(p_lg, p_rw, conv_w, conv_b, mu_p, n_ctx)

    w_bd = jnp.concatenate([_pair_block_diag(wa.reshape((-1,) + wa.shape[2:])),
                            _pair_block_diag(wx.reshape((-1,) + wx.shape[2:]))], axis=2)
    w_bd = w_bd.reshape(N_DIR, C // PAIR, PAIR, 2 * PAIR).astype(BF16)
    y = lru_scan(xc, w_bd, ba[:, None, :], bx[:, None, :], lam[:, None, :], n_ctx)

    zl = jnp.zeros_like(w2[0])
    w2p = jnp.stack([jnp.concatenate([w2[0], zl], axis=0), jnp.concatenate([zl, w2[1]], axis=0)])
    a2p = jnp.stack([jnp.concatenate([a2[0], zl], axis=0), jnp.concatenate([zl, a2[1]], axis=0)])
    yr, bo = rwkv_scan(rw, w0[:, None, :], w2p, a0[:, None, :], a2p,
                       k_k[None], k_a[None], r_k.reshape(1, -1), n_ctx)

    gd_cols = rw.shape[-1] - 3 * RWKV_WIDTH - N_DIR * (DECAY_LORA + AAA_LORA)
    g2p = jnp.pad(g2, ((0, gd_cols - g2.shape[0]), (0, 0))).astype(BF16)
    out = mixer_out(p_lg, y, yr, bo, rw, gn_g, gn_b, g2p, w_out.astype(BF16))
    return out if keep_ctx else out[:, n_ctx:]


def _peer(ht, wq, subkeys, u, v, layer):
    t1, s2, e1, e2 = peer_scores(ht, wq.T.astype(BF16), subkeys)
    ub, vt = peer_tables(u, v, layer)
    return peer_experts(ht, ub, vt, t1, s2, e1, e2)


def _seg_mods(mod, B, D):
    lat = mod[:B].reshape(B, 6, D)
    ctx = jnp.broadcast_to(mod[B].reshape(1, 6, D), (B, 6, D))
    both = jnp.stack([ctx, lat], axis=1)
    return [both[:, :, i, None, :] for i in range(6)]


def _forward(x, c, ctx, c_ctx, ada_w, ada_b, w_in, lru_conv_w, lru_conv_b, lru_wa, lru_ba,
             lru_wx, lru_bx, lru_lambda, rwkv_mu, rwkv_w0, rwkv_w2, rwkv_a0, rwkv_a2, rwkv_g2,
             rwkv_k_k, rwkv_k_a, rwkv_r_k, rwkv_gn_g, rwkv_gn_b, w_out, ln1_g, ln1_b,
             peer_wq, peer_subkeys, peer_u, peer_v, ln2_g, ln2_b):
    B, T, D = x.shape
    depth = ada_w.shape[0]
    alpha = (2.0 * depth) ** 0.25
    n_ctx = ctx.shape[1]
    x = x + _grid_pos_embed(T // GRID_W, D).astype(x.dtype)[None]
    cin = jnp.concatenate([jax.nn.silu(c), jax.nn.silu(c_ctx)[None],
                           jnp.zeros((8 - B - 1, D), F32)], axis=0)
    mods = [_seg_mods(matmul(cin, ada_w, layer=l) + ada_b[l], B, D) for l in range(depth)]
    z = jnp.concatenate([ctx, x], axis=1)
    sh1, sc1 = mods[0][0], mods[0][1]
    h = jnp.concatenate([_modulate(ctx, sh1[:, 0], sc1[:, 0]), _modulate(x, sh1[:, 1], sc1[:, 1])],
                        axis=1).astype(BF16)
    for l in range(depth):
        keep_ctx = l < depth - 1
        _, _, gt1, sh2, sc2, gt2 = mods[l]
        o = _token_mixer(h, n_ctx, keep_ctx, w_in[l], lru_conv_w[l], lru_conv_b[l], lru_wa[l],
                         lru_ba[l], lru_wx[l], lru_bx[l], lru_lambda[l], rwkv_mu[l], rwkv_w0[l],
                         rwkv_w2[l], rwkv_a0[l], rwkv_a2[l], rwkv_g2[l], rwkv_k_k[l], rwkv_k_a[l],
                         rwkv_r_k[l], rwkv_gn_g[l], rwkv_gn_b[l], w_out[l])
        if not keep_ctx:
            z, n_ctx = z[:, n_ctx:], 0
        z, ht = norm_mod(z, o, gt1, ln1_g[l], ln1_b[l], sh2, sc2, n_ctx, alpha, transpose=True)
        f = _peer(ht, peer_wq[l], peer_subkeys[l], peer_u, peer_v, l).reshape(z.shape)
        nxt = mods[min(l + 1, depth - 1)]
        z, h = norm_mod(z, f, gt2, ln2_g[l], ln2_b[l], nxt[0], nxt[1], n_ctx, alpha, transpose=False)
    return z


def kernel(x, c, ctx, c_ctx, ada_w, ada_b, w_in, lru_conv_w, lru_conv_b, lru_wa, lru_ba, lru_wx, lru_bx, lru_lambda, rwkv_mu, rwkv_w0, rwkv_w2, rwkv_a0, rwkv_a2, rwkv_g2, rwkv_k_k, rwkv_k_a, rwkv_r_k, rwkv_gn_g, rwkv_gn_b, w_out, ln1_g, ln1_b, peer_wq, peer_subkeys, peer_u, peer_v, ln2_g, ln2_b):
    return _forward(x, c, ctx, c_ctx, ada_w, ada_b, w_in, lru_conv_w, lru_conv_b, lru_wa, lru_ba,
                    lru_wx, lru_bx, lru_lambda, rwkv_mu, rwkv_w0, rwkv_w2, rwkv_a0, rwkv_a2, rwkv_g2,
                    rwkv_k_k, rwkv_k_a, rwkv_r_k, rwkv_gn_g, rwkv_gn_b, w_out, ln1_g, ln1_b,
                    peer_wq, peer_subkeys, peer_u, peer_v, ln2_g, ln2_b)
```

```python
import functools
import math

import jax
import jax.numpy as jnp
from jax import lax
from jax.experimental import pallas as pl
from jax.experimental.pallas import tpu as pltpu

F32 = jnp.float32
BF16 = jnp.bfloat16
HI = lax.Precision.HIGHEST

GRID_W = 64
N_DIR = 2
LRU_WIDTH = 1024
LRU_C = 8.0
CONV_WIDTH = 4
RWKV_WIDTH = 1024
HEAD_DIM = 64
DECAY_LORA = 64
AAA_LORA = 64
GATE_LORA = 160
PEER_HEADS = 8
PEER_N_KEYS = 128
PEER_TOPK = 16
LN_EPS = 1e-5
GN_EPS = 64e-5

LANES = 128
SUBLANES = 8
PAIR = LANES
CHUNK = 64
VMEM_LIMIT = 56 * 1024 * 1024


def _pick_block(n, target, align):
    best = None
    for b in range(align, min(n, target) + 1, align):
        if n % b == 0:
            best = b
    return best if best is not None else n


def _dot_nt(a, b):
    return lax.dot_general(a, b, (((1,), (1,)), ((), ())), preferred_element_type=F32)


def _dot_tn(a, b):
    return lax.dot_general(a, b, (((0,), (0,)), ((), ())), preferred_element_type=F32)


def _dot(a, b):
    return jnp.dot(a, b, preferred_element_type=F32)


def _dot_hi(a, b):
    return jnp.dot(a, b, precision=HI, preferred_element_type=F32)


def _split(x, n):
    out = []
    for _ in range(n - 1):
        p = x.astype(BF16)
        out.append(p)
        x = x - p.astype(F32)
    out.append(x.astype(BF16))
    return out


def _dot_exact_rhs(a, b, n):
    b = b.astype(BF16)
    acc = None
    for p in _split(a, n):
        d = _dot(p, b)
        acc = d if acc is None else acc + d
    return acc


def _dot_exact_lhs(a, b, n):
    a = a.astype(BF16)
    acc = None
    for p in _split(b, n):
        d = _dot(a, p)
        acc = d if acc is None else acc + d
    return acc


def _dot_3pass(a, b):
    ah, al = _split(a, 2)
    bh, bl = _split(b, 2)
    return _dot(ah, bh) + (_dot(ah, bl) + _dot(al, bh))


def _softplus(x):
    return jnp.maximum(x, 0.0) + jnp.log1p(jnp.exp(-jnp.abs(x)))


def _mm_body(x_ref, w_ref, o_ref):
    o_ref[...] = _dot(x_ref[...].astype(BF16), w_ref[...].astype(BF16)).astype(o_ref.dtype)


def matmul(x, w, col0=0, n_cols=None, layer=None, bm_target=1088, bn_target=512):
    M, K = x.shape
    N = w.shape[-1] - col0 if n_cols is None else n_cols
    bm = _pick_block(M, bm_target, 8)
    bn = _pick_block(math.gcd(N, col0) if col0 else N, bn_target, LANES)
    j0 = col0 // bn
    if layer is None:
        w_spec = pl.BlockSpec((K, bn), lambda i, j: (0, j0 + j))
    else:
        w_spec = pl.BlockSpec((None, K, bn), lambda i, j: (layer, 0, j0 + j))
    return pl.pallas_call(
        _mm_body,
        grid=(M // bm, N // bn),
        in_specs=[pl.BlockSpec((bm, K), lambda i, j: (i, 0)), w_spec],
        out_specs=pl.BlockSpec((bm, bn), lambda i, j: (i, j)),
        out_shape=jax.ShapeDtypeStruct((M, N), F32),
        compiler_params=pltpu.CompilerParams(
            dimension_semantics=("parallel", "parallel"), vmem_limit_bytes=VMEM_LIMIT),
        name="matmul",
    )(x, w)


def _prep_body(p_lru_ref, lo_lru_ref, hi_lru_ref, p_rw_ref, lo_rw_ref, hi_rw_ref,
               cw_ref, cb_ref, mu_ref, xc_ref, rw_ref, *, nb_ctx):
    t = pl.program_id(1)
    tb = p_lru_ref.shape[1]
    first = (t == 0) | (t == nb_ctx)
    last = (t == nb_ctx - 1) | (t == pl.num_programs(1) - 1)
    row = lax.broadcasted_iota(jnp.int32, (tb, 1), 0)

    def shifted(x, lo, hi, d):
        if d == 0:
            return x
        y = pltpu.roll(x, (-d) % tb, axis=0)
        if d < 0:
            for i in range(-d):
                fill = jnp.where(first, 0.0, lo[SUBLANES + d + i:SUBLANES + d + i + 1, :])
                y = jnp.where(row == i, fill, y)
        else:
            for i in range(d):
                fill = jnp.where(last, 0.0, hi[i:i + 1, :])
                y = jnp.where(row == tb - d + i, fill, y)
        return y

    x = p_lru_ref[0]
    lo, hi = lo_lru_ref[0], hi_lru_ref[0]
    acc = cb_ref[...] + jnp.zeros_like(x)
    for tap in range(CONV_WIDTH):
        acc = acc + shifted(x, lo, hi, tap - CONV_WIDTH // 2) * cw_ref[tap:tap + 1, :]
    xc_ref[0] = acc

    z = p_rw_ref[0]
    lo, hi = lo_rw_ref[0], hi_rw_ref[0]
    rw_ref[0] = z + mu_ref[0:1, :] * (shifted(z, lo, hi, -1) - z) + mu_ref[1:2, :] * (shifted(z, lo, hi, 1) - z)


def mixer_prep(p_lg, p_rw, conv_w, conv_b, mu_p, n_ctx):
    B, TT, RW = p_rw.shape
    C = LRU_WIDTH
    tb = _pick_block(math.gcd(n_ctx, TT - n_ctx), 256, SUBLANES)
    nb, nb_ctx, r8 = TT // tb, n_ctx // tb, tb // SUBLANES
    cur = lambda b, t: (b, t, 0)
    lo = lambda b, t: (b, jnp.maximum(t * r8 - 1, 0), 0)
    hi = lambda b, t: (b, jnp.minimum((t + 1) * r8, TT // SUBLANES - 1), 0)
    return pl.pallas_call(
        functools.partial(_prep_body, nb_ctx=nb_ctx),
        grid=(B, nb),
        in_specs=[pl.BlockSpec((1, tb, C), cur), pl.BlockSpec((1, SUBLANES, C), lo),
                  pl.BlockSpec((1, SUBLANES, C), hi),
                  pl.BlockSpec((1, tb, RW), cur), pl.BlockSpec((1, SUBLANES, RW), lo),
                  pl.BlockSpec((1, SUBLANES, RW), hi),
                  pl.BlockSpec(conv_w.shape, lambda b, t: (0, 0)),
                  pl.BlockSpec((1, C), lambda b, t: (0, 0)),
                  pl.BlockSpec(mu_p.shape, lambda b, t: (0, 0))],
        out_specs=[pl.BlockSpec((1, tb, C), cur), pl.BlockSpec((1, tb, RW), cur)],
        out_shape=[jax.ShapeDtypeStruct((B, TT, C), F32), jax.ShapeDtypeStruct((B, TT, RW), F32)],
        compiler_params=pltpu.CompilerParams(
            dimension_semantics=("parallel", "parallel"), vmem_limit_bytes=VMEM_LIMIT),
        name="mixer_prep",
    )(p_lg, p_lg, p_lg, p_rw, p_rw, p_rw, conv_w, conv_b[None], mu_p)


def _scan_block(g, t, n_batch, nb_ctx, nb):
    bwd = jnp.where(t < nb_ctx, nb_ctx - 1 - t, nb - 1 - (t - nb_ctx))
    return jnp.where(g // n_batch == 1, bwd, t)


def _lru_body(xc_ref, w_ref, ba_ref, bx_ref, lam_ref, y_ref, a_scr, b_scr, h_scr, *, n_batch):
    tb = xc_ref.shape[1]
    n_pair = xc_ref.shape[2] // PAIR
    rev = pl.program_id(0) // n_batch == 1

    @pl.when(pl.program_id(1) == 0)
    def _():
        h_scr[...] = jnp.zeros_like(h_scr)

    xc = xc_ref[0]
    ra, ia = [], []
    for p in range(n_pair):
        g = _dot(xc[:, p * PAIR:(p + 1) * PAIR].astype(BF16), w_ref[0, p])
        ra.append(g[:, :PAIR])
        ia.append(g[:, PAIR:])
    r = jax.nn.sigmoid(jnp.concatenate(ra, axis=1) + ba_ref[0])
    i = jax.nn.sigmoid(jnp.concatenate(ia, axis=1) + bx_ref[0])
    log_a = -LRU_C * r * _softplus(-lam_ref[0])
    a_scr[...] = jnp.exp(log_a)
    b_scr[...] = jnp.sqrt(jnp.maximum(1.0 - jnp.exp(2.0 * log_a), 0.0)) * (i * xc)

    def step(t, h):
        tt = jnp.where(rev, tb - 1 - t, t)
        h = a_scr[pl.ds(tt, 1), :] * h + b_scr[pl.ds(tt, 1), :]
        y_ref[0, pl.ds(tt, 1), :] = h
        return h

    h_scr[...] = lax.fori_loop(0, tb, step, h_scr[...], unroll=8)


def lru_scan(xc, w_bd, ba, bx, lam, n_ctx):
    B, TT, C = xc.shape
    tb = _pick_block(math.gcd(n_ctx, TT - n_ctx), 256, 8)
    nb, nb_ctx = TT // tb, n_ctx // tb
    seq = lambda g, t: (g % B, _scan_block(g, t, B, nb_ctx, nb), 0)
    dmap = lambda g, t: (g // B, 0, 0)
    return pl.pallas_call(
        functools.partial(_lru_body, n_batch=B),
        grid=(N_DIR * B, nb),
        in_specs=[pl.BlockSpec((1, tb, C), seq),
                  pl.BlockSpec((1, C // PAIR, PAIR, 2 * PAIR), lambda g, t: (g // B, 0, 0, 0)),
                  pl.BlockSpec((1, 1, C), dmap),
                  pl.BlockSpec((1, 1, C), dmap),
                  pl.BlockSpec((1, 1, C), dmap)],
        out_specs=pl.BlockSpec((1, tb, C), lambda g, t: (g, _scan_block(g, t, B, nb_ctx, nb), 0)),
        out_shape=jax.ShapeDtypeStruct((N_DIR * B, TT, C), F32),
        scratch_shapes=[pltpu.VMEM((tb, C), F32), pltpu.VMEM((tb, C), F32), pltpu.VMEM((1, C), F32)],
        compiler_params=pltpu.CompilerParams(
            dimension_semantics=("parallel", "arbitrary"), vmem_limit_bytes=VMEM_LIMIT),
        name="lru_scan",
    )(xc, w_bd, ba, bx, lam)


def _rwkv_body(r_ref, k_ref, v_ref, wd_ref, ad_ref, w0_ref, w2_ref, a0_ref, a2_ref,
               kk_ref, ka_ref, rk_ref, y_ref, bo_ref, s_ref, *, n_batch):
    L = r_ref.shape[1]
    C = r_ref.shape[2]
    n_pair = C // PAIR
    pairs = range(n_pair)
    sgn = jnp.where(pl.program_id(0) // n_batch == 1, -1, 1)

    @pl.when(pl.program_id(1) == 0)
    def _():
        s_ref[...] = jnp.zeros_like(s_ref)

    wpre = w0_ref[0] + _dot_3pass(jnp.tanh(wd_ref[0]), w2_ref[0])
    ld = -jnp.exp(-_softplus(-wpre) - 0.5)
    alr = jax.nn.sigmoid(a0_ref[0] + _dot_3pass(ad_ref[0], a2_ref[0]))

    ti = lax.broadcasted_iota(jnp.int32, (L, L), 0)
    tj = lax.broadcasted_iota(jnp.int32, (L, L), 1)
    cs = _dot_exact_lhs((((ti - tj) * sgn) >= 0).astype(F32), ld, 3)
    cs_end = jnp.where(sgn < 0, cs[0:1, :], cs[L - 1:L, :])

    ri = lax.broadcasted_iota(jnp.int32, (PAIR, PAIR), 0)
    ci = lax.broadcasted_iota(jnp.int32, (PAIR, PAIR), 1)
    same = (ri // HEAD_DIM) == (ci // HEAD_DIM)
    bd_ones = same.astype(F32)
    order = (ri - ci) * sgn
    strict = same & (order > 0)
    incl = same & (order >= 0)
    eye = (ri == ci).astype(F32)
    m0 = lax.broadcasted_iota(jnp.int32, (L, PAIR), 1) < HEAD_DIM

    def dup(z):
        return jnp.concatenate([z, z], axis=0)

    def slab(z):
        return jnp.where(m0, z[:L], z[L:])

    at, rt, v2, bh, kh, vv, lhs4, bt2, kt2, g_end = [], [], [], [], [], [], [], [], [], []
    for p in pairs:
        sl = slice(p * PAIR, (p + 1) * PAIR)
        r = r_ref[0, :, sl]
        k = k_ref[0, :, sl]
        v = v_ref[0, :, sl]
        ld_p, alr_p, cs_p, cs_l = ld[:, sl], alr[:, sl], cs[:, sl], cs_end[:, sl]
        kk0 = k * kk_ref[:, sl]
        ssq = _dot_exact_rhs(kk0 * kk0, bd_ones, 2)
        kk = kk0 * lax.rsqrt(jnp.maximum(ssq, 1e-24))
        kd = k * (1.0 + (alr_p - 1.0) * ka_ref[:, sl])
        b = kk * alr_p
        bo_ref[0, :, sl] = _dot_exact_rhs(r * kd * rk_ref[:, sl], bd_ones, 2) * v
        g_inv = jnp.exp(-cs_p)
        g_rel = jnp.exp(cs_l - cs_p)
        at_p = -kk * jnp.exp(cs_p - ld_p)
        rt_p = r * jnp.exp(cs_p)
        zero = jnp.zeros_like(at_p)
        lhs4.append(jnp.concatenate([jnp.where(m0, at_p, zero), jnp.where(m0, zero, at_p),
                                     jnp.where(m0, rt_p, zero), jnp.where(m0, zero, rt_p)], axis=0).astype(BF16))
        bt2.append(dup(b * g_inv).astype(BF16))
        kt2.append(dup(kd * g_inv).astype(BF16))
        at.append(at_p)
        rt.append(rt_p)
        vv.append(v)
        v2.append(dup(v).astype(BF16))
        bh.append((b * g_rel).astype(BF16))
        kh.append((kd * g_rel).astype(BF16))
        g_end.append(jnp.exp(cs_l))

    ab = [_dot_nt(lhs4[p], bt2[p]) for p in pairs]
    ak = [_dot_nt(lhs4[p], kt2[p]) for p in pairs]
    a_ab = [jnp.where(strict, ab[p][:2 * L], 0.0) for p in pairs]
    a_rb = [jnp.where(incl, ab[p][2 * L:], 0.0).astype(BF16) for p in pairs]
    a_ak = [jnp.where(strict, ak[p][:2 * L], 0.0).astype(BF16) for p in pairs]
    a_rk = [jnp.where(incl, ak[p][2 * L:], 0.0).astype(BF16) for p in pairs]
    av = [_dot(a_ak[p], v2[p]) for p in pairs]
    kv = [_dot(a_rk[p], v2[p]) for p in pairs]

    def off_diag(blk):
        rb = ((ri % (2 * blk)) >= blk).astype(jnp.int32)
        cb = ((ci % (2 * blk)) >= blk).astype(jnp.int32)
        return same & ((ri // (2 * blk)) == (ci // (2 * blk))) & (((rb - cb) * sgn) == 1)

    m1 = off_diag(1)
    tinv = [eye + jnp.where(m1, a_ab[p], 0.0) for p in pairs]
    blk = 2
    while blk < L:
        mb = off_diag(blk)
        tb = [tinv[p].astype(BF16) for p in pairs]
        t1 = [_dot(tb[p], jnp.where(mb, a_ab[p], 0.0).astype(BF16)).astype(BF16) for p in pairs]
        tinv = [tinv[p] + _dot(t1[p], tb[p]) for p in pairs]
        blk *= 2

    x = [_dot(tinv[p].astype(BF16), jnp.concatenate([dup(at[p]), av[p]], axis=1).astype(BF16)) for p in pairs]
    abar = [slab(x[p][:, :PAIR]) for p in pairs]
    uv = [slab(x[p][:, PAIR:]) for p in pairs]
    z = [_dot(a_rb[p], jnp.concatenate([dup(abar[p]), dup(uv[p])], axis=1).astype(BF16)) for p in pairs]
    rbar = [(rt[p] + slab(z[p][:, :PAIR])).astype(BF16) for p in pairs]
    yv = [slab(z[p][:, PAIR:] + kv[p]) for p in pairs]
    mp = [jnp.where(same, _dot_tn(abar[p].astype(BF16), bh[p]), 0.0).astype(BF16) for p in pairs]
    sv = [jnp.where(same, _dot_tn(jnp.concatenate([uv[p], vv[p]], axis=0).astype(BF16),
                                  jnp.concatenate([bh[p], kh[p]], axis=0)), 0.0) for p in pairs]
    for p in pairs:
        sl = slice(p * PAIR, (p + 1) * PAIR)
        s0 = s_ref[p]
        s0b = s0.astype(BF16)
        y_ref[0, :, sl] = _dot_nt(rbar[p], s0b) + yv[p]
        s_ref[p] = s0 * g_end[p] + _dot(s0b, mp[p]) + sv[p]


def rwkv_scan(rw, w0, w2p, a0, a2p, k_k, k_a, r_k, n_ctx):
    B, TT, _ = rw.shape
    C = RWKV_WIDTH
    L = CHUNK
    assert n_ctx % L == 0 and TT % L == 0 and 2 * L == PAIR and C % PAIR == 0
    nb, nb_ctx = TT // L, n_ctx // L
    dmap3 = lambda g, c: (g // B, 0, 0)
    cmap = lambda g, c: (0, 0)
    seq = lambda j: pl.BlockSpec((1, L, C), lambda g, c: (g % B, _scan_block(g, c, B, nb_ctx, nb), j))
    lora = lambda j: pl.BlockSpec((1, L, PAIR), lambda g, c: (g % B, _scan_block(g, c, B, nb_ctx, nb), j))
    out = pl.BlockSpec((1, L, C), lambda g, c: (g, _scan_block(g, c, B, nb_ctx, nb), 0))
    return pl.pallas_call(
        functools.partial(_rwkv_body, n_batch=B),
        grid=(N_DIR * B, nb),
        in_specs=[seq(0), seq(1), seq(2), lora(3 * C // PAIR), lora(3 * C // PAIR + 1),
                  pl.BlockSpec((1, 1, C), dmap3), pl.BlockSpec((1, PAIR, C), dmap3),
                  pl.BlockSpec((1, 1, C), dmap3), pl.BlockSpec((1, PAIR, C), dmap3),
                  pl.BlockSpec((1, C), cmap), pl.BlockSpec((1, C), cmap), pl.BlockSpec((1, C), cmap)],
        out_specs=[out, out],
        out_shape=[jax.ShapeDtypeStruct((N_DIR * B, TT, C), F32)] * 2,
        scratch_shapes=[pltpu.VMEM((C // PAIR, PAIR, PAIR), F32)],
        compiler_params=pltpu.CompilerParams(
            dimension_semantics=("parallel", "arbitrary"), vmem_limit_bytes=VMEM_LIMIT),
        name="rwkv_scan",
    )(rw, rw, rw, rw, rw, w0, w2p, a0, a2p, k_k, k_a, r_k)


def _merge_body(pg_ref, yf_ref, yb_ref, rf_ref, rb_ref, bf_ref, bb_ref, gd_ref,
                gng_ref, gnb_ref, g2_ref, wo_ref, o_ref):
    left = jax.nn.gelu(pg_ref[0]) * (yf_ref[0] + yb_ref[0])
    ys = rf_ref[0] + rb_ref[0]
    ri = lax.broadcasted_iota(jnp.int32, (PAIR, PAIR), 0)
    ci = lax.broadcasted_iota(jnp.int32, (PAIR, PAIR), 1)
    bd_ones = ((ri // HEAD_DIM) == (ci // HEAD_DIM)).astype(F32)
    yn = []
    for p in range(ys.shape[1] // PAIR):
        sl = slice(p * PAIR, (p + 1) * PAIR)
        yp = ys[:, sl]
        d = yp - _dot_exact_rhs(yp, bd_ones, 3) * (1.0 / HEAD_DIM)
        var = _dot_exact_rhs(d * d, bd_ones, 3) * (1.0 / HEAD_DIM)
        yn.append(d * lax.rsqrt(var + GN_EPS))
    yn = jnp.concatenate(yn, axis=1) * gng_ref[...] + gnb_ref[...]
    gate = _dot(jax.nn.sigmoid(gd_ref[0]).astype(BF16), g2_ref[...])
    right = (yn + (bf_ref[0] + bb_ref[0])) * gate
    merged = jnp.concatenate([left, right], axis=1).astype(BF16)
    o_ref[0] = _dot(merged, wo_ref[...])


def mixer_out(p_lg, y, yr, bo, rw, gn_g, gn_b, g2p, w_out):
    B, TT, _ = p_lg.shape
    C = LRU_WIDTH
    GW = g2p.shape[0]
    D = w_out.shape[1]
    bm = _pick_block(TT, 256, SUBLANES)
    fwd = pl.BlockSpec((1, bm, C), lambda b, t: (b, t, 0))
    bwd = pl.BlockSpec((1, bm, C), lambda b, t: (B + b, t, 0))
    const = lambda shape: pl.BlockSpec(shape, lambda b, t: (0, 0))
    return pl.pallas_call(
        _merge_body,
        grid=(B, TT // bm),
        in_specs=[pl.BlockSpec((1, bm, C), lambda b, t: (b, t, 1)),
                  fwd, bwd, fwd, bwd, fwd, bwd,
                  pl.BlockSpec((1, bm, GW), lambda b, t: (b, t, rw.shape[2] // GW - 1)),
                  const((1, C)), const((1, C)), const(g2p.shape), const(w_out.shape)],
        out_specs=pl.BlockSpec((1, bm, D), lambda b, t: (b, t, 0)),
        out_shape=jax.ShapeDtypeStruct((B, TT, D), F32),
        compiler_params=pltpu.CompilerParams(
            dimension_semantics=("parallel", "parallel"), vmem_limit_bytes=VMEM_LIMIT),
        name="mixer_out",
    )(p_lg, y, y, yr, yr, bo, bo, rw, gn_g[None], gn_b[None], g2p, w_out)


def _norm_mod_body(z_ref, o_ref, gate_ref, g_ref, b_ref, sh_ref, sc_ref, zo_ref, h_ref, *, alpha, transpose):
    z = alpha * z_ref[0] + gate_ref[0, 0] * o_ref[0]
    mu = jnp.mean(z, axis=-1, keepdims=True)
    d = z - mu
    var = jnp.mean(d * d, axis=-1, keepdims=True)
    zn = d * lax.rsqrt(var + LN_EPS) * g_ref[...] + b_ref[...]
    zo_ref[0] = zn
    h = zn * (1.0 + sc_ref[0, 0]) + sh_ref[0, 0]
    if transpose:
        h_ref[...] = h.T.astype(BF16)
    else:
        h_ref[0] = h.astype(BF16)


def norm_mod(z, o, gate, ln_g, ln_b, shift, scale, n_ctx, alpha, transpose):
    B, TT, D = z.shape
    tb = _pick_block(math.gcd(n_ctx, TT - n_ctx) if n_ctx else TT, 256, LANES)
    nb, nb_ctx = TT // tb, n_ctx // tb
    cur = pl.BlockSpec((1, tb, D), lambda b, t: (b, t, 0))
    seg = pl.BlockSpec((1, 1, 1, D), lambda b, t: (b, jnp.where(t >= nb_ctx, 1, 0), 0, 0))
    vec = pl.BlockSpec((1, D), lambda b, t: (0, 0))
    if transpose:
        h_spec = pl.BlockSpec((D, tb), lambda b, t: (0, b * nb + t))
        h_shape = jax.ShapeDtypeStruct((D, B * TT), BF16)
    else:
        h_spec, h_shape = cur, jax.ShapeDtypeStruct((B, TT, D), BF16)
    return pl.pallas_call(
        functools.partial(_norm_mod_body, alpha=alpha, transpose=transpose),
        grid=(B, nb),
        in_specs=[cur, cur, seg, vec, vec, seg, seg],
        out_specs=[cur, h_spec],
        out_shape=[jax.ShapeDtypeStruct((B, TT, D), F32), h_shape],
        compiler_params=pltpu.CompilerParams(
            dimension_semantics=("parallel", "parallel"), vmem_limit_bytes=VMEM_LIMIT),
        name="norm_mod",
    )(z, o, gate, ln_g[None], ln_b[None], shift, scale)


def _top_values(x, n):
    rows = lax.broadcasted_iota(jnp.int32, x.shape, 0).astype(F32)
    out = []
    for _ in range(n):
        m = jnp.max(x, axis=0, keepdims=True)
        out.append(m)
        first = jnp.min(jnp.where(x == m, rows, float(x.shape[0])), axis=0, keepdims=True)
        x = jnp.where(rows == first, -jnp.inf, x)
    return out


def _candidate_rows(t1, t2, n):
    k = len(t1)
    t1c = jnp.concatenate(t1, axis=0)
    t2c = jnp.concatenate(t2, axis=0)
    row = lax.broadcasted_iota(jnp.int32, (SUBLANES, t1c.shape[1]), 0)
    groups = []
    for j in range(k):
        cnt = min(k, n // (j + 1))
        if cnt <= 1:
            break
        for g in range(0, cnt, SUBLANES):
            piece = t1c[g:g + SUBLANES] + t2[j]
            groups.append(piece if cnt - g >= SUBLANES else jnp.where(row < cnt - g, piece, -jnp.inf))
    j0 = j
    for g in range(j0, k, SUBLANES):
        piece = t1[0] + t2c[g:g + SUBLANES]
        groups.append(piece if k - g >= SUBLANES else jnp.where(row < k - g, piece, -jnp.inf))
    return jnp.concatenate(groups, axis=0)


def _top_values_distinct(x, n):
    start = jnp.sum((x == -jnp.inf).astype(F32), axis=0, keepdims=True)
    out = []
    for _ in range(n):
        m = jnp.max(x, axis=0, keepdims=True)
        out.append(m)
        x = jnp.where(x == m, -jnp.inf, x)
    return out, jnp.sum((x == -jnp.inf).astype(F32), axis=0, keepdims=True) - start


def _peer_score_body(ht_ref, wqt_ref, sk_ref, t1_ref, s2_ref, e1_ref, e2_ref):
    n_heads = sk_ref.shape[0]
    dq = sk_ref.shape[3]
    qt = _dot(wqt_ref[...], ht_ref[...])
    scores = [[_dot_hi(sk_ref[h, half], qt[(2 * h + half) * dq:(2 * h + half + 1) * dq, :])
               for half in range(2)] for h in range(n_heads)]

    def emit(h, s, t1, t2, top):
        zsum = jnp.exp(top[0] - top[0])
        for t in top[1:PEER_TOPK]:
            zsum = zsum + jnp.exp(t - top[0])
        tau = 0.5 * (top[PEER_TOPK - 1] + top[PEER_TOPK])
        t1_ref[h] = jnp.where(s[0] >= t1[-1], tau - s[0], jnp.inf)
        s2_ref[h] = jnp.where(s[1] >= t2[-1], s[1], -jnp.inf)
        e1_ref[h] = jnp.exp(s[0] - t1[0]) / zsum
        e2_ref[h] = jnp.exp(s[1] - t2[0])

    excess = jnp.zeros((1, qt.shape[1]), F32)
    for h in range(n_heads):
        s = scores[h]
        t1, n1 = _top_values_distinct(s[0], PEER_TOPK)
        t2, n2 = _top_values_distinct(s[1], PEER_TOPK)
        top, n3 = _top_values_distinct(_candidate_rows(t1, t2, PEER_TOPK + 1), PEER_TOPK + 1)
        excess = excess + (n1 - PEER_TOPK) + (n2 - PEER_TOPK) + (n3 - (PEER_TOPK + 1))
        emit(h, s, t1, t2, top)

    @pl.when(jnp.max(excess) > 0.0)
    def _():
        for h in range(n_heads):
            s = scores[h]
            t1 = _top_values(s[0], PEER_TOPK)
            t2 = _top_values(s[1], PEER_TOPK)
            emit(h, s, t1, t2, _top_values(_candidate_rows(t1, t2, PEER_TOPK + 1), PEER_TOPK + 1))


def peer_scores(ht, wqt, subkeys):
    D, M = ht.shape
    nh, _, nk, dq = subkeys.shape
    tb = _pick_block(M, 256, LANES)
    big = pl.BlockSpec((nh, nk, tb), lambda i: (0, 0, i))
    shp = jax.ShapeDtypeStruct((nh, nk, M), F32)
    return pl.pallas_call(
        _peer_score_body,
        grid=(M // tb,),
        in_specs=[pl.BlockSpec((D, tb), lambda i: (0, i)),
                  pl.BlockSpec(wqt.shape, lambda i: (0, 0)),
                  pl.BlockSpec(subkeys.shape, lambda i: (0, 0, 0, 0))],
        out_specs=[big, big, big, big],
        out_shape=[shp, shp, shp, shp],
        compiler_params=pltpu.CompilerParams(
            dimension_semantics=("parallel",), vmem_limit_bytes=VMEM_LIMIT),
        name="peer_scores",
    )(ht, wqt, subkeys)


def _tables_body(u_ref, v_ref, ub_ref, vt_ref):
    ub_ref[...] = u_ref[...].astype(BF16)
    vt_ref[...] = v_ref[...].T.astype(BF16)


def peer_tables(u, v, layer):
    _, E, D = u.shape
    eb = _pick_block(E, 512, LANES)
    src = pl.BlockSpec((None, eb, D), lambda i: (layer, i, 0))
    return pl.pallas_call(
        _tables_body,
        grid=(E // eb,),
        in_specs=[src, src],
        out_specs=[pl.BlockSpec((eb, D), lambda i: (i, 0)), pl.BlockSpec((D, eb), lambda i: (0, i))],
        out_shape=[jax.ShapeDtypeStruct((E, D), BF16), jax.ShapeDtypeStruct((D, E), BF16)],
        compiler_params=pltpu.CompilerParams(
            dimension_semantics=("parallel",), vmem_limit_bytes=VMEM_LIMIT),
        name="peer_tables",
    )(u, v)


E2_TILE = 64


def _peer_expert_body(ht_ref, u_ref, vt_ref, t1_ref, s2_ref, e1_ref, e2_ref, o_ref, act_scr, x_scr, ot_scr):
    j = pl.program_id(1)
    n_heads, nk, tb = s2_ref.shape
    eb = u_ref.shape[0]
    per = eb // nk

    @pl.when(j == 0)
    def _():
        ot_scr[...] = jnp.zeros_like(ot_scr)

    act_scr[...] = _dot(u_ref[...], ht_ref[...])
    t1full = [[t1_ref[h, pl.ds(j * per + c, 1), :] for h in range(n_heads)] for c in range(per)]
    e1full = [[e1_ref[h, pl.ds(j * per + c, 1), :] for h in range(n_heads)] for c in range(per)]
    for tj in range(tb // LANES):
        tsl = slice(tj * LANES, (tj + 1) * LANES)
        t1rows = [[r[:, tsl] for r in rows] for rows in t1full]
        e1rows = [[r[:, tsl] for r in rows] for rows in e1full]
        for et in range(nk // E2_TILE):
            esl = slice(et * E2_TILE, (et + 1) * E2_TILE)
            accs = [jnp.zeros((E2_TILE, LANES), F32) for _ in range(per)]
            for h in range(n_heads):
                s2t = s2_ref[h, esl, tsl]
                e2t = e2_ref[h, esl, tsl]
                for c in range(per):
                    accs[c] = accs[c] + jnp.where(s2t >= t1rows[c][h], e2t, 0.0) * e1rows[c][h]
            for c in range(per):
                rsl = slice(c * nk + et * E2_TILE, c * nk + (et + 1) * E2_TILE)
                a = act_scr[rsl, tsl]
                gelu = 0.5 * a * (1.0 + lax.erf(a * (2.0 ** -0.5)))
                x_scr[rsl, tsl] = (accs[c] * gelu).astype(BF16)
    ot_scr[...] += _dot(vt_ref[...], x_scr[...])

    @pl.when(j == pl.num_programs(1) - 1)
    def _():
        o_ref[...] = ot_scr[...].T


def peer_experts(ht, u, vt, t1, s2, e1, e2):
    D, M = ht.shape
    nh, nk, _ = s2.shape
    E = u.shape[0]
    tb = _pick_block(M, 512, LANES)
    eb = _pick_block(E, 1024, nk)
    big = pl.BlockSpec((nh, nk, tb), lambda i, j: (0, 0, i))
    return pl.pallas_call(
        _peer_expert_body,
        grid=(M // tb, E // eb),
        in_specs=[pl.BlockSpec((D, tb), lambda i, j: (0, i)),
                  pl.BlockSpec((eb, D), lambda i, j: (j, 0)),
                  pl.BlockSpec((D, eb), lambda i, j: (0, j)),
                  big, big, big, big],
        out_specs=pl.BlockSpec((tb, D), lambda i, j: (i, 0)),
        out_shape=jax.ShapeDtypeStruct((M, D), F32),
        scratch_shapes=[pltpu.VMEM((eb, tb), F32), pltpu.VMEM((eb, tb), BF16), pltpu.VMEM((D, tb), F32)],
        compiler_params=pltpu.CompilerParams(
            dimension_semantics=("parallel", "arbitrary"), vmem_limit_bytes=VMEM_LIMIT),
        name="peer_experts",
    )(ht, u, vt, t1, s2, e1, e2)


def _modulate(z, shift, scale):
    return z * (1.0 + scale) + shift


def _grid_pos_embed(rows, dim):
    t = jnp.arange(rows * GRID_W)
    row = (t // GRID_W).astype(F32)
    col = (t % GRID_W).astype(F32)
    quarter = dim // 4
    freq = 1.0 / (10000.0 ** (jnp.arange(quarter, dtype=F32) / quarter))

    def sincos(p):
        ang = p[:, None] * freq[None, :]
        return jnp.concatenate([jnp.sin(ang), jnp.cos(ang)], axis=-1)

    return jnp.concatenate([sincos(row), sincos(col)], axis=-1)


def _pair_block_diag(w):
    nh = w.shape[0]
    w = w.reshape(nh // 2, 2, HEAD_DIM, HEAD_DIM)
    z = jnp.zeros_like(w[:, 0])
    top = jnp.concatenate([w[:, 0], z], axis=2)
    bot = jnp.concatenate([z, w[:, 1]], axis=2)
    return jnp.concatenate([top, bot], axis=1)


def _token_mixer(h_all, n_ctx, keep_ctx, w_in, conv_w, conv_b, wa, ba, wx, bx, lam,
                 mu, w0, w2, a0, a2, g2, k_k, k_a, r_k, gn_g, gn_b, w_out):
    B, TT, D = h_all.shape
    C = LRU_WIDTH
    in_cols = w_in.shape[1]
    pad_cols = -in_cols % 512
    w_in_p = jnp.pad(w_in, ((0, 0), (0, pad_cols))).astype(BF16)
    hb = h_all.astype(BF16).reshape(B * TT, D)
    p_lg = matmul(hb, w_in_p, 0, 2 * C).reshape(B, TT, -1)
    p_rw = matmul(hb, w_in_p, 2 * C).reshape(B, TT, -1)
    mu_p = jnp.pad(mu, ((0, 0), (0, p_rw.shape[-1] - mu.shape[1])))
    xc, rw = mixer_prep(p_lg, p_rw, conv_w, conv_b, mu_p, n_ctx)

    w_bd = jnp.concatenate([_pair_block_diag(wa.reshape((-1,) + wa.shape[2:])),
                            _pair_block_diag(wx.reshape((-1,) + wx.shape[2:]))], axis=2)
    w_bd = w_bd.reshape(N_DIR, C // PAIR, PAIR, 2 * PAIR).astype(BF16)
    y = lru_scan(xc, w_bd, ba[:, None, :], bx[:, None, :], lam[:, None, :], n_ctx)

    zl = jnp.zeros_like(w2[0])
    w2p = jnp.stack([jnp.concatenate([w2[0], zl], axis=0), jnp.concatenate([zl, w2[1]], axis=0)])
    a2p = jnp.stack([jnp.concatenate([a2[0], zl], axis=0), jnp.concatenate([zl, a2[1]], axis=0)])
    yr, bo = rwkv_scan(rw, w0[:, None, :], w2p, a0[:, None, :], a2p,
                       k_k[None], k_a[None], r_k.reshape(1, -1), n_ctx)

    gd_cols = rw.shape[-1] - 3 * RWKV_WIDTH - N_DIR * (DECAY_LORA + AAA_LORA)
    g2p = jnp.pad(g2, ((0, gd_cols - g2.shape[0]), (0, 0))).astype(BF16)
    out = mixer_out(p_lg, y, yr, bo, rw, gn_g, gn_b, g2p, w_out.astype(BF16))
    return out if keep_ctx else out[:, n_ctx:]


def _peer(ht, wq, subkeys, u, v, layer):
    t1, s2, e1, e2 = peer_scores(ht, wq.T.astype(BF16), subkeys)
    ub, vt = peer_tables(u, v, layer)
    return peer_experts(ht, ub, vt, t1, s2, e1, e2)


def _seg_mods(mod, B, D):
    lat = mod[:B].reshape(B, 6, D)
    ctx = jnp.broadcast_to(mod[B].reshape(1, 6, D), (B, 6, D))
    both = jnp.stack([ctx, lat], axis=1)
    return [both[:, :, i, None, :] for i in range(6)]


def _forward(x, c, ctx, c_ctx, ada_w, ada_b, w_in, lru_conv_w, lru_conv_b, lru_wa, lru_ba,
             lru_wx, lru_bx, lru_lambda, rwkv_mu, rwkv_w0, rwkv_w2, rwkv_a0, rwkv_a2, rwkv_g2,
             rwkv_k_k, rwkv_k_a, rwkv_r_k, rwkv_gn_g, rwkv_gn_b, w_out, ln1_g, ln1_b,
             peer_wq, peer_subkeys, peer_u, peer_v, ln2_g, ln2_b):
    B, T, D = x.shape
    depth = ada_w.shape[0]
    alpha = (2.0 * depth) ** 0.25
    n_ctx = ctx.shape[1]
    x = x + _grid_pos_embed(T // GRID_W, D).astype(x.dtype)[None]
    cin = jnp.concatenate([jax.nn.silu(c), jax.nn.silu(c_ctx)[None],
                           jnp.zeros((8 - B - 1, D), F32)], axis=0)
    mods = [_seg_mods(matmul(cin, ada_w, layer=l) + ada_b[l], B, D) for l in range(depth)]
    z = jnp.concatenate([ctx, x], axis=1)
    sh1, sc1 = mods[0][0], mods[0][1]
    h = jnp.concatenate([_modulate(ctx, sh1[:, 0], sc1[:, 0]), _modulate(x, sh1[:, 1], sc1[:, 1])],
                        axis=1).astype(BF16)
    for l in range(depth):
        keep_ctx = l < depth - 1
        _, _, gt1, sh2, sc2, gt2 = mods[l]
        o = _token_mixer(h, n_ctx, keep_ctx, w_in[l], lru_conv_w[l], lru_conv_b[l], lru_wa[l],
                         lru_ba[l], lru_wx[l], lru_bx[l], lru_lambda[l], rwkv_mu[l], rwkv_w0[l],
                         rwkv_w2[l], rwkv_a0[l], rwkv_a2[l], rwkv_g2[l], rwkv_k_k[l], rwkv_k_a[l],
                         rwkv_r_k[l], rwkv_gn_g[l], rwkv_gn_b[l], w_out[l])
        if not keep_ctx:
            z, n_ctx = z[:, n_ctx:], 0
        z, ht = norm_mod(z, o, gt1, ln1_g[l], ln1_b[l], sh2, sc2, n_ctx, alpha, transpose=True)
        f = _peer(ht, peer_wq[l], peer_subkeys[l], peer_u, peer_v, l).reshape(z.shape)
        nxt = mods[min(l + 1, depth - 1)]
        z, h = norm_mod(z, f, gt2, ln2_g[l], ln2_b[l], nxt[0], nxt[1], n_ctx, alpha, transpose=False)
    return z


def kernel(x, c, ctx, c_ctx, ada_w, ada_b, w_in, lru_conv_w, lru_conv_b, lru_wa, lru_ba, lru_wx, lru_bx, lru_lambda, rwkv_mu, rwkv_w0, rwkv_w2, rwkv_a0, rwkv_a2, rwkv_g2, rwkv_k_k, rwkv_k_a, rwkv_r_k, rwkv_gn_g, rwkv_gn_b, w_out, ln1_g, ln1_b, peer_wq, peer_subkeys, peer_u, peer_v, ln2_g, ln2_b):
    return _forward(x, c, ctx, c_ctx, ada_w, ada_b, w_in, lru_conv_w, lru_conv_b, lru_wa, lru_ba,
                    lru_wx, lru_bx, lru_lambda, rwkv_mu, rwkv_w0, rwkv_w2, rwkv_a0, rwkv_a2, rwkv_g2,
                    rwkv_k_k, rwkv_k_a, rwkv_r_k, rwkv_gn_g, rwkv_gn_b, w_out, ln1_g, ln1_b,
                    peer_wq, peer_subkeys, peer_u, peer_v, ln2_g, ln2_b)
```

```python
import functools
import math

import jax
import jax.numpy as jnp
from jax import lax
from jax.experimental import pallas as pl
from jax.experimental.pallas import tpu as pltpu

F32 = jnp.float32
BF16 = jnp.bfloat16
HI = lax.Precision.HIGHEST

GRID_W = 64
N_DIR = 2
LRU_WIDTH = 1024
LRU_C = 8.0
CONV_WIDTH = 4
RWKV_WIDTH = 1024
HEAD_DIM = 64
DECAY_LORA = 64
AAA_LORA = 64
GATE_LORA = 160
PEER_HEADS = 8
PEER_N_KEYS = 128
PEER_TOPK = 16
LN_EPS = 1e-5
GN_EPS = 64e-5

LANES = 128
SUBLANES = 8
MXU_DIM = 256
PAIR = LANES
CHUNK = 64
VMEM_LIMIT = 56 * 1024 * 1024


def _pick_block(n, target, align):
    best = None
    for b in range(align, min(n, target) + 1, align):
        if n % b == 0:
            best = b
    return best if best is not None else n


def _dot_nt(a, b):
    return lax.dot_general(a, b, (((1,), (1,)), ((), ())), preferred_element_type=F32)


def _dot_tn(a, b):
    return lax.dot_general(a, b, (((0,), (0,)), ((), ())), preferred_element_type=F32)


def _dot(a, b):
    return jnp.dot(a, b, preferred_element_type=F32)


def _dot_hi(a, b):
    return jnp.dot(a, b, precision=HI, preferred_element_type=F32)


def _split(x, n):
    out = []
    for _ in range(n - 1):
        p = x.astype(BF16)
        out.append(p)
        x = x - p.astype(F32)
    out.append(x.astype(BF16))
    return out


def _dot_exact_rhs(a, b, n):
    b = b.astype(BF16)
    acc = None
    for p in _split(a, n):
        d = _dot(p, b)
        acc = d if acc is None else acc + d
    return acc


def _dot_exact_lhs(a, b, n):
    a = a.astype(BF16)
    acc = None
    for p in _split(b, n):
        d = _dot(a, p)
        acc = d if acc is None else acc + d
    return acc


def _dot_3pass(a, b):
    ah, al = _split(a, 2)
    bh, bl = _split(b, 2)
    return _dot(ah, bh) + (_dot(ah, bl) + _dot(al, bh))


def _softplus(x):
    return jnp.maximum(x, 0.0) + jnp.log1p(jnp.exp(-jnp.abs(x)))


def _mm_body(x_ref, w_ref, o_ref):
    o_ref[...] = _dot(x_ref[...].astype(BF16), w_ref[...].astype(BF16)).astype(o_ref.dtype)


def matmul(x, w, col0=0, n_cols=None, layer=None, bm_target=1088, bn_target=512):
    M, K = x.shape
    N = w.shape[-1] - col0 if n_cols is None else n_cols
    bm = _pick_block(M, bm_target, 8)
    bn = _pick_block(math.gcd(N, col0) if col0 else N, bn_target, LANES)
    j0 = col0 // bn
    if layer is None:
        w_spec = pl.BlockSpec((K, bn), lambda i, j: (0, j0 + j))
    else:
        w_spec = pl.BlockSpec((None, K, bn), lambda i, j: (layer, 0, j0 + j))
    return pl.pallas_call(
        _mm_body,
        grid=(M // bm, N // bn),
        in_specs=[pl.BlockSpec((bm, K), lambda i, j: (i, 0)), w_spec],
        out_specs=pl.BlockSpec((bm, bn), lambda i, j: (i, j)),
        out_shape=jax.ShapeDtypeStruct((M, N), F32),
        compiler_params=pltpu.CompilerParams(
            dimension_semantics=("parallel", "parallel"), vmem_limit_bytes=VMEM_LIMIT),
        name="matmul",
    )(x, w)


def _prep_body(p_lru_ref, lo_lru_ref, hi_lru_ref, p_rw_ref, lo_rw_ref, hi_rw_ref,
               cw_ref, cb_ref, mu_ref, xc_ref, rw_ref, *, nb_ctx):
    t = pl.program_id(1)
    tb = p_lru_ref.shape[1]
    first = (t == 0) | (t == nb_ctx)
    last = (t == nb_ctx - 1) | (t == pl.num_programs(1) - 1)
    row = lax.broadcasted_iota(jnp.int32, (tb, 1), 0)

    def shifted(x, lo, hi, d):
        if d == 0:
            return x
        y = pltpu.roll(x, (-d) % tb, axis=0)
        if d < 0:
            for i in range(-d):
                fill = jnp.where(first, 0.0, lo[SUBLANES + d + i:SUBLANES + d + i + 1, :])
                y = jnp.where(row == i, fill, y)
        else:
            for i in range(d):
                fill = jnp.where(last, 0.0, hi[i:i + 1, :])
                y = jnp.where(row == tb - d + i, fill, y)
        return y

    x = p_lru_ref[0]
    lo, hi = lo_lru_ref[0], hi_lru_ref[0]
    acc = cb_ref[...] + jnp.zeros_like(x)
    for tap in range(CONV_WIDTH):
        acc = acc + shifted(x, lo, hi, tap - CONV_WIDTH // 2) * cw_ref[tap:tap + 1, :]
    xc_ref[0] = acc

    z = p_rw_ref[0]
    lo, hi = lo_rw_ref[0], hi_rw_ref[0]
    rw_ref[0] = z + mu_ref[0:1, :] * (shifted(z, lo, hi, -1) - z) + mu_ref[1:2, :] * (shifted(z, lo, hi, 1) - z)


def mixer_prep(p_lg, p_rw, conv_w, conv_b, mu_p, n_ctx):
    B, TT, RW = p_rw.shape
    C = LRU_WIDTH
    tb = _pick_block(math.gcd(n_ctx, TT - n_ctx), 256, SUBLANES)
    nb, nb_ctx, r8 = TT // tb, n_ctx // tb, tb // SUBLANES
    cur = lambda b, t: (b, t, 0)
    lo = lambda b, t: (b, jnp.maximum(t * r8 - 1, 0), 0)
    hi = lambda b, t: (b, jnp.minimum((t + 1) * r8, TT // SUBLANES - 1), 0)
    return pl.pallas_call(
        functools.partial(_prep_body, nb_ctx=nb_ctx),
        grid=(B, nb),
        in_specs=[pl.BlockSpec((1, tb, C), cur), pl.BlockSpec((1, SUBLANES, C), lo),
                  pl.BlockSpec((1, SUBLANES, C), hi),
                  pl.BlockSpec((1, tb, RW), cur), pl.BlockSpec((1, SUBLANES, RW), lo),
                  pl.BlockSpec((1, SUBLANES, RW), hi),
                  pl.BlockSpec(conv_w.shape, lambda b, t: (0, 0)),
                  pl.BlockSpec((1, C), lambda b, t: (0, 0)),
                  pl.BlockSpec(mu_p.shape, lambda b, t: (0, 0))],
        out_specs=[pl.BlockSpec((1, tb, C), cur), pl.BlockSpec((1, tb, RW), cur)],
        out_shape=[jax.ShapeDtypeStruct((B, TT, C), F32), jax.ShapeDtypeStruct((B, TT, RW), F32)],
        compiler_params=pltpu.CompilerParams(
            dimension_semantics=("parallel", "parallel"), vmem_limit_bytes=VMEM_LIMIT),
        name="mixer_prep",
    )(p_lg, p_lg, p_lg, p_rw, p_rw, p_rw, conv_w, conv_b[None], mu_p)


def _scan_block(g, t, n_batch, nb_ctx, nb):
    bwd = jnp.where(t < nb_ctx, nb_ctx - 1 - t, nb - 1 - (t - nb_ctx))
    return jnp.where(g // n_batch == 1, bwd, t)


def _lru_body(xc_ref, w_ref, ba_ref, bx_ref, lam_ref, y_ref, a_scr, b_scr, h_scr, *, n_batch):
    tb = xc_ref.shape[1]
    n_pair = xc_ref.shape[2] // PAIR
    rev = pl.program_id(0) // n_batch == 1

    @pl.when(pl.program_id(1) == 0)
    def _():
        h_scr[...] = jnp.zeros_like(h_scr)

    xc = xc_ref[0]
    ra, ia = [], []
    for p in range(n_pair):
        g = _dot(xc[:, p * PAIR:(p + 1) * PAIR].astype(BF16), w_ref[0, p])
        ra.append(g[:, :PAIR])
        ia.append(g[:, PAIR:])
    r = jax.nn.sigmoid(jnp.concatenate(ra, axis=1) + ba_ref[0])
    i = jax.nn.sigmoid(jnp.concatenate(ia, axis=1) + bx_ref[0])
    log_a = -LRU_C * r * _softplus(-lam_ref[0])
    a_scr[...] = jnp.exp(log_a)
    b_scr[...] = jnp.sqrt(jnp.maximum(1.0 - jnp.exp(2.0 * log_a), 0.0)) * (i * xc)

    def step(t, h):
        tt = jnp.where(rev, tb - 1 - t, t)
        h = a_scr[pl.ds(tt, 1), :] * h + b_scr[pl.ds(tt, 1), :]
        y_ref[0, pl.ds(tt, 1), :] = h
        return h

    h_scr[...] = lax.fori_loop(0, tb, step, h_scr[...], unroll=8)


def lru_scan(xc, w_bd, ba, bx, lam, n_ctx):
    B, TT, C = xc.shape
    tb = _pick_block(math.gcd(n_ctx, TT - n_ctx), 256, 8)
    nb, nb_ctx = TT // tb, n_ctx // tb
    seq = lambda g, t: (g % B, _scan_block(g, t, B, nb_ctx, nb), 0)
    dmap = lambda g, t: (g // B, 0, 0)
    return pl.pallas_call(
        functools.partial(_lru_body, n_batch=B),
        grid=(N_DIR * B, nb),
        in_specs=[pl.BlockSpec((1, tb, C), seq),
                  pl.BlockSpec((1, C // PAIR, PAIR, 2 * PAIR), lambda g, t: (g // B, 0, 0, 0)),
                  pl.BlockSpec((1, 1, C), dmap),
                  pl.BlockSpec((1, 1, C), dmap),
                  pl.BlockSpec((1, 1, C), dmap)],
        out_specs=pl.BlockSpec((1, tb, C), lambda g, t: (g, _scan_block(g, t, B, nb_ctx, nb), 0)),
        out_shape=jax.ShapeDtypeStruct((N_DIR * B, TT, C), F32),
        scratch_shapes=[pltpu.VMEM((tb, C), F32), pltpu.VMEM((tb, C), F32), pltpu.VMEM((1, C), F32)],
        compiler_params=pltpu.CompilerParams(
            dimension_semantics=("parallel", "arbitrary"), vmem_limit_bytes=VMEM_LIMIT),
        name="lru_scan",
    )(xc, w_bd, ba, bx, lam)


def _rwkv_body(r_ref, k_ref, v_ref, wd_ref, ad_ref, w0_ref, w2_ref, a0_ref, a2_ref,
               kk_ref, ka_ref, rk_ref, y_ref, bo_ref, s_ref, *, n_batch):
    L = r_ref.shape[1]
    C = r_ref.shape[2]
    n_pair = C // PAIR
    pairs = range(n_pair)
    sgn = jnp.where(pl.program_id(0) // n_batch == 1, -1, 1)

    @pl.when(pl.program_id(1) == 0)
    def _():
        s_ref[...] = jnp.zeros_like(s_ref)

    wpre = w0_ref[0] + _dot_3pass(jnp.tanh(wd_ref[0]), w2_ref[0])
    ld = -jnp.exp(-_softplus(-wpre) - 0.5)
    alr = jax.nn.sigmoid(a0_ref[0] + _dot_3pass(ad_ref[0], a2_ref[0]))

    ti = lax.broadcasted_iota(jnp.int32, (L, L), 0)
    tj = lax.broadcasted_iota(jnp.int32, (L, L), 1)
    cs = _dot_exact_lhs((((ti - tj) * sgn) >= 0).astype(F32), ld, 3)
    cs_end = jnp.where(sgn < 0, cs[0:1, :], cs[L - 1:L, :])

    ri = lax.broadcasted_iota(jnp.int32, (PAIR, PAIR), 0)
    ci = lax.broadcasted_iota(jnp.int32, (PAIR, PAIR), 1)
    same = (ri // HEAD_DIM) == (ci // HEAD_DIM)
    bd_ones = same.astype(F32)
    order = (ri - ci) * sgn
    strict = same & (order > 0)
    incl = same & (order >= 0)
    eye = (ri == ci).astype(F32)
    m0 = lax.broadcasted_iota(jnp.int32, (L, PAIR), 1) < HEAD_DIM

    def dup(z):
        return jnp.concatenate([z, z], axis=0)

    def slab(z):
        return jnp.where(m0, z[:L], z[L:])

    at, rt, v2, bh, kh, vv, lhs4, bt2, kt2, g_end = [], [], [], [], [], [], [], [], [], []
    for p in pairs:
        sl = slice(p * PAIR, (p + 1) * PAIR)
        r = r_ref[0, :, sl]
        k = k_ref[0, :, sl]
        v = v_ref[0, :, sl]
        ld_p, alr_p, cs_p, cs_l = ld[:, sl], alr[:, sl], cs[:, sl], cs_end[:, sl]
        kk0 = k * kk_ref[:, sl]
        ssq = _dot_exact_rhs(kk0 * kk0, bd_ones, 2)
        kk = kk0 * lax.rsqrt(jnp.maximum(ssq, 1e-24))
        kd = k * (1.0 + (alr_p - 1.0) * ka_ref[:, sl])
        b = kk * alr_p
        bo_ref[0, :, sl] = _dot_exact_rhs(r * kd * rk_ref[:, sl], bd_ones, 2) * v
        g_inv = jnp.exp(-cs_p)
        g_rel = jnp.exp(cs_l - cs_p)
        at_p = -kk * jnp.exp(cs_p - ld_p)
        rt_p = r * jnp.exp(cs_p)
        zero = jnp.zeros_like(at_p)
        lhs4.append(jnp.concatenate([jnp.where(m0, at_p, zero), jnp.where(m0, zero, at_p),
                                     jnp.where(m0, rt_p, zero), jnp.where(m0, zero, rt_p)], axis=0).astype(BF16))
        bt2.append(dup(b * g_inv).astype(BF16))
        kt2.append(dup(kd * g_inv).astype(BF16))
        at.append(at_p)
        rt.append(rt_p)
        vv.append(v)
        v2.append(dup(v).astype(BF16))
        bh.append((b * g_rel).astype(BF16))
        kh.append((kd * g_rel).astype(BF16))
        g_end.append(jnp.exp(cs_l))

    ab = [_dot_nt(lhs4[p], bt2[p]) for p in pairs]
    ak = [_dot_nt(lhs4[p], kt2[p]) for p in pairs]
    a_ab = [jnp.where(strict, ab[p][:2 * L], 0.0) for p in pairs]
    a_rb = [jnp.where(incl, ab[p][2 * L:], 0.0).astype(BF16) for p in pairs]
    a_ak = [jnp.where(strict, ak[p][:2 * L], 0.0).astype(BF16) for p in pairs]
    a_rk = [jnp.where(incl, ak[p][2 * L:], 0.0).astype(BF16) for p in pairs]
    av = [_dot(a_ak[p], v2[p]) for p in pairs]
    kv = [_dot(a_rk[p], v2[p]) for p in pairs]

    def off_diag(blk):
        rb = ((ri % (2 * blk)) >= blk).astype(jnp.int32)
        cb = ((ci % (2 * blk)) >= blk).astype(jnp.int32)
        return same & ((ri // (2 * blk)) == (ci // (2 * blk))) & (((rb - cb) * sgn) == 1)

    m1 = off_diag(1)
    tinv = [eye + jnp.where(m1, a_ab[p], 0.0) for p in pairs]
    blk = 2
    while blk < L:
        mb = off_diag(blk)
        tb = [tinv[p].astype(BF16) for p in pairs]
        t1 = [_dot(tb[p], jnp.where(mb, a_ab[p], 0.0).astype(BF16)).astype(BF16) for p in pairs]
        tinv = [tinv[p] + _dot(t1[p], tb[p]) for p in pairs]
        blk *= 2

    x = [_dot(tinv[p].astype(BF16), jnp.concatenate([dup(at[p]), av[p]], axis=1).astype(BF16)) for p in pairs]
    abar = [slab(x[p][:, :PAIR]) for p in pairs]
    uv = [slab(x[p][:, PAIR:]) for p in pairs]
    z = [_dot(a_rb[p], jnp.concatenate([dup(abar[p]), dup(uv[p])], axis=1).astype(BF16)) for p in pairs]
    rbar = [(rt[p] + slab(z[p][:, :PAIR])).astype(BF16) for p in pairs]
    yv = [slab(z[p][:, PAIR:] + kv[p]) for p in pairs]
    mp = [jnp.where(same, _dot_tn(abar[p].astype(BF16), bh[p]), 0.0).astype(BF16) for p in pairs]
    sv = [jnp.where(same, _dot_tn(jnp.concatenate([uv[p], vv[p]], axis=0).astype(BF16),
                                  jnp.concatenate([bh[p], kh[p]], axis=0)), 0.0) for p in pairs]
    for p in pairs:
        sl = slice(p * PAIR, (p + 1) * PAIR)
        s0 = s_ref[p]
        s0b = s0.astype(BF16)
        y_ref[0, :, sl] = _dot_nt(rbar[p], s0b) + yv[p]
        s_ref[p] = s0 * g_end[p] + _dot(s0b, mp[p]) + sv[p]


def rwkv_scan(rw, w0, w2p, a0, a2p, k_k, k_a, r_k, n_ctx):
    B, TT, _ = rw.shape
    C = RWKV_WIDTH
    L = CHUNK
    assert n_ctx % L == 0 and TT % L == 0 and 2 * L == PAIR and C % PAIR == 0
    nb, nb_ctx = TT // L, n_ctx // L
    dmap3 = lambda g, c: (g // B, 0, 0)
    cmap = lambda g, c: (0, 0)
    seq = lambda j: pl.BlockSpec((1, L, C), lambda g, c: (g % B, _scan_block(g, c, B, nb_ctx, nb), j))
    lora = lambda j: pl.BlockSpec((1, L, PAIR), lambda g, c: (g % B, _scan_block(g, c, B, nb_ctx, nb), j))
    out = pl.BlockSpec((1, L, C), lambda g, c: (g, _scan_block(g, c, B, nb_ctx, nb), 0))
    return pl.pallas_call(
        functools.partial(_rwkv_body, n_batch=B),
        grid=(N_DIR * B, nb),
        in_specs=[seq(0), seq(1), seq(2), lora(3 * C // PAIR), lora(3 * C // PAIR + 1),
                  pl.BlockSpec((1, 1, C), dmap3), pl.BlockSpec((1, PAIR, C), dmap3),
                  pl.BlockSpec((1, 1, C), dmap3), pl.BlockSpec((1, PAIR, C), dmap3),
                  pl.BlockSpec((1, C), cmap), pl.BlockSpec((1, C), cmap), pl.BlockSpec((1, C), cmap)],
        out_specs=[out, out],
        out_shape=[jax.ShapeDtypeStruct((N_DIR * B, TT, C), F32)] * 2,
        scratch_shapes=[pltpu.VMEM((C // PAIR, PAIR, PAIR), F32)],
        compiler_params=pltpu.CompilerParams(
            dimension_semantics=("parallel", "arbitrary"), vmem_limit_bytes=VMEM_LIMIT),
        name="rwkv_scan",
    )(rw, rw, rw, rw, rw, w0, w2p, a0, a2p, k_k, k_a, r_k)


def _merge_body(pg_ref, yf_ref, yb_ref, rf_ref, rb_ref, bf_ref, bb_ref, gd_ref,
                gng_ref, gnb_ref, g2_ref, wo_ref, o_ref):
    left = jax.nn.gelu(pg_ref[0]) * (yf_ref[0] + yb_ref[0])
    ys = rf_ref[0] + rb_ref[0]
    ri = lax.broadcasted_iota(jnp.int32, (PAIR, PAIR), 0)
    ci = lax.broadcasted_iota(jnp.int32, (PAIR, PAIR), 1)
    bd_ones = ((ri // HEAD_DIM) == (ci // HEAD_DIM)).astype(F32)
    yn = []
    for p in range(ys.shape[1] // PAIR):
        sl = slice(p * PAIR, (p + 1) * PAIR)
        yp = ys[:, sl]
        d = yp - _dot_exact_rhs(yp, bd_ones, 3) * (1.0 / HEAD_DIM)
        var = _dot_exact_rhs(d * d, bd_ones, 3) * (1.0 / HEAD_DIM)
        yn.append(d * lax.rsqrt(var + GN_EPS))
    yn = jnp.concatenate(yn, axis=1) * gng_ref[...] + gnb_ref[...]
    gate = _dot(jax.nn.sigmoid(gd_ref[0]).astype(BF16), g2_ref[...])
    right = (yn + (bf_ref[0] + bb_ref[0])) * gate
    merged = jnp.concatenate([left, right], axis=1).astype(BF16)
    o_ref[0] = _dot(merged, wo_ref[...])


def mixer_out(p_lg, y, yr, bo, rw, gn_g, gn_b, g2p, w_out):
    B, TT, _ = p_lg.shape
    C = LRU_WIDTH
    GW = g2p.shape[0]
    D = w_out.shape[1]
    bm = _pick_block(TT, 256, SUBLANES)
    fwd = pl.BlockSpec((1, bm, C), lambda b, t: (b, t, 0))
    bwd = pl.BlockSpec((1, bm, C), lambda b, t: (B + b, t, 0))
    const = lambda shape: pl.BlockSpec(shape, lambda b, t: (0, 0))
    return pl.pallas_call(
        _merge_body,
        grid=(B, TT // bm),
        in_specs=[pl.BlockSpec((1, bm, C), lambda b, t: (b, t, 1)),
                  fwd, bwd, fwd, bwd, fwd, bwd,
                  pl.BlockSpec((1, bm, GW), lambda b, t: (b, t, rw.shape[2] // GW - 1)),
                  const((1, C)), const((1, C)), const(g2p.shape), const(w_out.shape)],
        out_specs=pl.BlockSpec((1, bm, D), lambda b, t: (b, t, 0)),
        out_shape=jax.ShapeDtypeStruct((B, TT, D), F32),
        compiler_params=pltpu.CompilerParams(
            dimension_semantics=("parallel", "parallel"), vmem_limit_bytes=VMEM_LIMIT),
        name="mixer_out",
    )(p_lg, y, y, yr, yr, bo, bo, rw, gn_g[None], gn_b[None], g2p, w_out)


def _norm_mod_body(z_ref, o_ref, gate_ref, g_ref, b_ref, sh_ref, sc_ref, zo_ref, h_ref, *, alpha, transpose):
    z = alpha * z_ref[0] + gate_ref[0, 0] * o_ref[0]
    mu = jnp.mean(z, axis=-1, keepdims=True)
    d = z - mu
    var = jnp.mean(d * d, axis=-1, keepdims=True)
    zn = d * lax.rsqrt(var + LN_EPS) * g_ref[...] + b_ref[...]
    zo_ref[0] = zn
    h = zn * (1.0 + sc_ref[0, 0]) + sh_ref[0, 0]
    if transpose:
        h_ref[...] = h.T.astype(BF16)
    else:
        h_ref[0] = h.astype(BF16)


def norm_mod(z, o, gate, ln_g, ln_b, shift, scale, n_ctx, alpha, transpose):
    B, TT, D = z.shape
    tb = _pick_block(math.gcd(n_ctx, TT - n_ctx) if n_ctx else TT, 256, LANES)
    nb, nb_ctx = TT // tb, n_ctx // tb
    cur = pl.BlockSpec((1, tb, D), lambda b, t: (b, t, 0))
    seg = pl.BlockSpec((1, 1, 1, D), lambda b, t: (b, jnp.where(t >= nb_ctx, 1, 0), 0, 0))
    vec = pl.BlockSpec((1, D), lambda b, t: (0, 0))
    if transpose:
        h_spec = pl.BlockSpec((D, tb), lambda b, t: (0, b * nb + t))
        h_shape = jax.ShapeDtypeStruct((D, B * TT), BF16)
    else:
        h_spec, h_shape = cur, jax.ShapeDtypeStruct((B, TT, D), BF16)
    return pl.pallas_call(
        functools.partial(_norm_mod_body, alpha=alpha, transpose=transpose),
        grid=(B, nb),
        in_specs=[cur, cur, seg, vec, vec, seg, seg],
        out_specs=[cur, h_spec],
        out_shape=[jax.ShapeDtypeStruct((B, TT, D), F32), h_shape],
        compiler_params=pltpu.CompilerParams(
            dimension_semantics=("parallel", "parallel"), vmem_limit_bytes=VMEM_LIMIT),
        name="norm_mod",
    )(z, o, gate, ln_g[None], ln_b[None], shift, scale)


def _top_values(x, n):
    rows = lax.broadcasted_iota(jnp.int32, x.shape, 0).astype(F32)
    out = []
    for _ in range(n):
        m = jnp.max(x, axis=0, keepdims=True)
        out.append(m)
        first = jnp.min(jnp.where(x == m, rows, float(x.shape[0])), axis=0, keepdims=True)
        x = jnp.where(rows == first, -jnp.inf, x)
    return out


def _candidate_rows(t1, t2, n):
    k = len(t1)
    t1c = jnp.concatenate(t1, axis=0)
    t2c = jnp.concatenate(t2, axis=0)
    row = lax.broadcasted_iota(jnp.int32, (SUBLANES, t1c.shape[1]), 0)
    groups = []
    for j in range(k):
        cnt = min(k, n // (j + 1))
        if cnt <= 1:
            break
        for g in range(0, cnt, SUBLANES):
            piece = t1c[g:g + SUBLANES] + t2[j]
            groups.append(piece if cnt - g >= SUBLANES else jnp.where(row < cnt - g, piece, -jnp.inf))
    j0 = j
    for g in range(j0, k, SUBLANES):
        piece = t1[0] + t2c[g:g + SUBLANES]
        groups.append(piece if k - g >= SUBLANES else jnp.where(row < k - g, piece, -jnp.inf))
    return jnp.concatenate(groups, axis=0)


def _top_values_distinct(x, n):
    start = jnp.sum((x == -jnp.inf).astype(F32), axis=0, keepdims=True)
    out = []
    for _ in range(n):
        m = jnp.max(x, axis=0, keepdims=True)
        out.append(m)
        x = jnp.where(x == m, -jnp.inf, x)
    return out, jnp.sum((x == -jnp.inf).astype(F32), axis=0, keepdims=True) - start


def _peer_score_body(ht_ref, wqt_ref, sk_ref, t1_ref, s2_ref, e1_ref, e2_ref):
    n_heads = sk_ref.shape[0]
    dq = sk_ref.shape[3]
    qt = _dot(wqt_ref[...], ht_ref[...])
    scores = [[_dot_hi(sk_ref[h, half], qt[(2 * h + half) * dq:(2 * h + half + 1) * dq, :])
               for half in range(2)] for h in range(n_heads)]

    def emit(h, s, t1, t2, top):
        zsum = jnp.exp(top[0] - top[0])
        for t in top[1:PEER_TOPK]:
            zsum = zsum + jnp.exp(t - top[0])
        tau = 0.5 * (top[PEER_TOPK - 1] + top[PEER_TOPK])
        t1_ref[h] = jnp.where(s[0] >= t1[-1], tau - s[0], jnp.inf)
        s2_ref[h] = jnp.where(s[1] >= t2[-1], s[1], -jnp.inf)
        e1_ref[h] = jnp.exp(s[0] - t1[0]) / zsum
        e2_ref[h] = jnp.exp(s[1] - t2[0])

    excess = jnp.zeros((1, qt.shape[1]), F32)
    for h in range(n_heads):
        s = scores[h]
        t1, n1 = _top_values_distinct(s[0], PEER_TOPK)
        t2, n2 = _top_values_distinct(s[1], PEER_TOPK)
        top, n3 = _top_values_distinct(_candidate_rows(t1, t2, PEER_TOPK + 1), PEER_TOPK + 1)
        excess = excess + (n1 - PEER_TOPK) + (n2 - PEER_TOPK) + (n3 - (PEER_TOPK + 1))
        emit(h, s, t1, t2, top)

    @pl.when(jnp.max(excess) > 0.0)
    def _():
        for h in range(n_heads):
            s = scores[h]
            t1 = _top_values(s[0], PEER_TOPK)
            t2 = _top_values(s[1], PEER_TOPK)
            emit(h, s, t1, t2, _top_values(_candidate_rows(t1, t2, PEER_TOPK + 1), PEER_TOPK + 1))


def peer_scores(ht, wqt, subkeys):
    D, M = ht.shape
    nh, _, nk, dq = subkeys.shape
    tb = _pick_block(M, 256, LANES)
    big = pl.BlockSpec((nh, nk, tb), lambda i: (0, 0, i))
    shp = jax.ShapeDtypeStruct((nh, nk, M), F32)
    return pl.pallas_call(
        _peer_score_body,
        grid=(M // tb,),
        in_specs=[pl.BlockSpec((D, tb), lambda i: (0, i)),
                  pl.BlockSpec(wqt.shape, lambda i: (0, 0)),
                  pl.BlockSpec(subkeys.shape, lambda i: (0, 0, 0, 0))],
        out_specs=[big, big, big, big],
        out_shape=[shp, shp, shp, shp],
        compiler_params=pltpu.CompilerParams(
            dimension_semantics=("parallel",), vmem_limit_bytes=VMEM_LIMIT),
        name="peer_scores",
    )(ht, wqt, subkeys)


def _tables_body(u_ref, v_ref, ub_ref, vt_ref):
    ub_ref[...] = u_ref[...].astype(BF16)
    vt_ref[...] = v_ref[...].T.astype(BF16)


def peer_tables(u, v, layer):
    _, E, D = u.shape
    eb = _pick_block(E, 512, LANES)
    src = pl.BlockSpec((None, eb, D), lambda i: (layer, i, 0))
    return pl.pallas_call(
        _tables_body,
        grid=(E // eb,),
        in_specs=[src, src],
        out_specs=[pl.BlockSpec((eb, D), lambda i: (i, 0)), pl.BlockSpec((D, eb), lambda i: (0, i))],
        out_shape=[jax.ShapeDtypeStruct((E, D), BF16), jax.ShapeDtypeStruct((D, E), BF16)],
        compiler_params=pltpu.CompilerParams(
            dimension_semantics=("parallel",), vmem_limit_bytes=VMEM_LIMIT),
        name="peer_tables",
    )(u, v)


E2_TILE = 64


def _peer_expert_body(ht_ref, u_ref, vt_ref, t1_ref, s2_ref, e1_ref, e2_ref, o_ref, act_scr, x_scr, ot_scr, w_scr):
    j = pl.program_id(1)
    n_heads, nk, tb = s2_ref.shape
    eb = u_ref.shape[0]
    per = eb // nk

    @pl.when(j == 0)
    def _():
        ot_scr[...] = jnp.zeros_like(ot_scr)

    t1full = [[t1_ref[h, pl.ds(j * per + c, 1), :] for h in range(n_heads)] for c in range(per)]
    e1full = [[e1_ref[h, pl.ds(j * per + c, 1), :] for h in range(n_heads)] for c in range(per)]

    def zero_of(v):
        bits = lax.bitcast_convert_type(v[:SUBLANES, :LANES].astype(F32), jnp.uint32)
        z = lax.shift_right_logical(lax.shift_right_logical(bits, jnp.uint32(16)), jnp.uint32(16))
        return lax.bitcast_convert_type(z, F32)

    def weights_tile(tj, et, z):
        tsl = slice(tj * LANES, (tj + 1) * LANES)
        esl = slice(et * E2_TILE, (et + 1) * E2_TILE)
        z64 = jnp.concatenate([z] * (E2_TILE // SUBLANES), axis=0)
        accs = [z64 for _ in range(per)]
        for h in range(n_heads):
            s2t = s2_ref[h, esl, tsl]
            e2t = e2_ref[h, esl, tsl]
            for c in range(per):
                accs[c] = accs[c] + jnp.where(s2t >= t1full[c][h][:, tsl], e2t, 0.0) * e1full[c][h][:, tsl]
        for c in range(per):
            w_scr[c * nk + et * E2_TILE:c * nk + (et + 1) * E2_TILE, tsl] = accs[c]
        return accs[0]

    kt = MXU_DIM
    n_k = u_ref.shape[1] // kt
    halves = [slice(0, tb // 2), slice(tb // 2, tb)]
    pieces = [(hs, k) for hs in halves for k in range(n_k)]
    tiles = [(tj, et) for tj in range(tb // LANES) for et in range(nk // E2_TILE)]
    per_tile = -(-len(pieces) // len(tiles))
    z_tile = jnp.zeros((SUBLANES, LANES), F32)
    z_piece = jnp.zeros((SUBLANES, LANES), F32)
    for i, (tj, et) in enumerate(tiles):
        z_next = z_piece
        for hs, k in pieces[i * per_tile:(i + 1) * per_tile]:
            rhs = ht_ref[k * kt:(k + 1) * kt, hs]
            zb = jnp.tile(z_tile, (rhs.shape[0] // SUBLANES, rhs.shape[1] // LANES)).astype(BF16)
            d = _dot(u_ref[:, k * kt:(k + 1) * kt], rhs + zb)
            if k == 0:
                act_scr[:, hs] = d
            else:
                act_scr[:, hs] += d
            z_next = zero_of(d)
        z_tile = zero_of(weights_tile(tj, et, z_piece))
        z_piece = z_next
    a = act_scr[...]
    gelu = 0.5 * a * (1.0 + lax.erf(a * (2.0 ** -0.5)))
    x_scr[...] = (w_scr[...] * gelu).astype(BF16)
    ot_scr[...] += _dot(vt_ref[...], x_scr[...])

    @pl.when(j == pl.num_programs(1) - 1)
    def _():
        o_ref[...] = ot_scr[...].T


def peer_experts(ht, u, vt, t1, s2, e1, e2):
    D, M = ht.shape
    nh, nk, _ = s2.shape
    E = u.shape[0]
    tb = _pick_block(M, 512, LANES)
    eb = _pick_block(E, 1024, nk)
    big = pl.BlockSpec((nh, nk, tb), lambda i, j: (0, 0, i))
    return pl.pallas_call(
        _peer_expert_body,
        grid=(M // tb, E // eb),
        in_specs=[pl.BlockSpec((D, tb), lambda i, j: (0, i)),
                  pl.BlockSpec((eb, D), lambda i, j: (j, 0)),
                  pl.BlockSpec((D, eb), lambda i, j: (0, j)),
                  big, big, big, big],
        out_specs=pl.BlockSpec((tb, D), lambda i, j: (i, 0)),
        out_shape=jax.ShapeDtypeStruct((M, D), F32),
        scratch_shapes=[pltpu.VMEM((eb, tb), F32), pltpu.VMEM((eb, tb), BF16), pltpu.VMEM((D, tb), F32),
                        pltpu.VMEM((eb, tb), F32)],
        compiler_params=pltpu.CompilerParams(
            dimension_semantics=("parallel", "arbitrary"), vmem_limit_bytes=VMEM_LIMIT),
        name="peer_experts",
    )(ht, u, vt, t1, s2, e1, e2)


def _modulate(z, shift, scale):
    return z * (1.0 + scale) + shift


def _grid_pos_embed(rows, dim):
    t = jnp.arange(rows * GRID_W)
    row = (t // GRID_W).astype(F32)
    col = (t % GRID_W).astype(F32)
    quarter = dim // 4
    freq = 1.0 / (10000.0 ** (jnp.arange(quarter, dtype=F32) / quarter))

    def sincos(p):
        ang = p[:, None] * freq[None, :]
        return jnp.concatenate([jnp.sin(ang), jnp.cos(ang)], axis=-1)

    return jnp.concatenate([sincos(row), sincos(col)], axis=-1)


def _pair_block_diag(w):
    nh = w.shape[0]
    w = w.reshape(nh // 2, 2, HEAD_DIM, HEAD_DIM)
    z = jnp.zeros_like(w[:, 0])
    top = jnp.concatenate([w[:, 0], z], axis=2)
    bot = jnp.concatenate([z, w[:, 1]], axis=2)
    return jnp.concatenate([top, bot], axis=1)


def _token_mixer(h_all, n_ctx, keep_ctx, w_in, conv_w, conv_b, wa, ba, wx, bx, lam,
                 mu, w0, w2, a0, a2, g2, k_k, k_a, r_k, gn_g, gn_b, w_out):
    B, TT, D = h_all.shape
    C = LRU_WIDTH
    in_cols = w_in.shape[1]
    pad_cols = -in_cols % 512
    w_in_p = jnp.pad(w_in, ((0, 0), (0, pad_cols))).astype(BF16)
    hb = h_all.astype(BF16).reshape(B * TT, D)
    p_lg = matmul(hb, w_in_p, 0, 2 * C).reshape(B, TT, -1)
    p_rw = matmul(hb, w_in_p, 2 * C).reshape(B, TT, -1)
    mu_p = jnp.pad(mu, ((0, 0), (0, p_rw.shape[-1] - mu.shape[1])))
    xc, rw = mixer_prep(p_lg, p_rw, conv_w, conv_b, mu_p, n_ctx)

    w_bd = jnp.concatenate([_pair_block_diag(wa.reshape((-1,) + wa.shape[2:])),
                            _pair_block_diag(wx.reshape((-1,) + wx.shape[2:]))], axis=2)
    w_bd = w_bd.reshape(N_DIR, C // PAIR, PAIR, 2 * PAIR).astype(BF16)
    y = lru_scan(xc, w_bd, ba[:, None, :], bx[:, None, :], lam[:, None, :], n_ctx)

    zl = jnp.zeros_like(w2[0])
    w2p = jnp.stack([jnp.concatenate([w2[0], zl], axis=0), jnp.concatenate([zl, w2[1]], axis=0)])
    a2p = jnp.stack([jnp.concatenate([a2[0], zl], axis=0), jnp.concatenate([zl, a2[1]], axis=0)])
    yr, bo = rwkv_scan(rw, w0[:, None, :], w2p, a0[:, None, :], a2p,
                       k_k[None], k_a[None], r_k.reshape(1, -1), n_ctx)

    gd_cols = rw.shape[-1] - 3 * RWKV_WIDTH - N_DIR * (DECAY_LORA + AAA_LORA)
    g2p = jnp.pad(g2, ((0, gd_cols - g2.shape[0]), (0, 0))).astype(BF16)
    out = mixer_out(p_lg, y, yr, bo, rw, gn_g, gn_b, g2p, w_out.astype(BF16))
    return out if keep_ctx else out[:, n_ctx:]


def _peer(ht, wq, subkeys, u, v, layer):
    t1, s2, e1, e2 = peer_scores(ht, wq.T.astype(BF16), subkeys)
    ub, vt = peer_tables(u, v, layer)
    return peer_experts(ht, ub, vt, t1, s2, e1, e2)


def _seg_mods(mod, B, D):
    lat = mod[:B].reshape(B, 6, D)
    ctx = jnp.broadcast_to(mod[B].reshape(1, 6, D), (B, 6, D))
    both = jnp.stack([ctx, lat], axis=1)
    return [both[:, :, i, None, :] for i in range(6)]


def _forward(x, c, ctx, c_ctx, ada_w, ada_b, w_in, lru_conv_w, lru_conv_b, lru_wa, lru_ba,
             lru_wx, lru_bx, lru_lambda, rwkv_mu, rwkv_w0, rwkv_w2, rwkv_a0, rwkv_a2, rwkv_g2,
             rwkv_k_k, rwkv_k_a, rwkv_r_k, rwkv_gn_g, rwkv_gn_b, w_out, ln1_g, ln1_b,
             peer_wq, peer_subkeys, peer_u, peer_v, ln2_g, ln2_b):
    B, T, D = x.shape
    depth = ada_w.shape[0]
    alpha = (2.0 * depth) ** 0.25
    n_ctx = ctx.shape[1]
    x = x + _grid_pos_embed(T // GRID_W, D).astype(x.dtype)[None]
    cin = jnp.concatenate([jax.nn.silu(c), jax.nn.silu(c_ctx)[None],
                           jnp.zeros((8 - B - 1, D), F32)], axis=0)
    mods = [_seg_mods(matmul(cin, ada_w, layer=l) + ada_b[l], B, D) for l in range(depth)]
    z = jnp.concatenate([ctx, x], axis=1)
    sh1, sc1 = mods[0][0], mods[0][1]
    h = jnp.concatenate([_modulate(ctx, sh1[:, 0], sc1[:, 0]), _modulate(x, sh1[:, 1], sc1[:, 1])],
                        axis=1).astype(BF16)
    for l in range(depth):
        keep_ctx = l < depth - 1
        _, _, gt1, sh2, sc2, gt2 = mods[l]
        o = _token_mixer(h, n_ctx, keep_ctx, w_in[l], lru_conv_w[l], lru_conv_b[l], lru_wa[l],
                         lru_ba[l], lru_wx[l], lru_bx[l], lru_lambda[l], rwkv_mu[l], rwkv_w0[l],
                         rwkv_w2[l], rwkv_a0[l], rwkv_a2[l], rwkv_g2[l], rwkv_k_k[l], rwkv_k_a[l],
                         rwkv_r_k[l], rwkv_gn_g[l], rwkv_gn_b[l], w_out[l])
        if not keep_ctx:
            z, n_ctx = z[:, n_ctx:], 0
        z, ht = norm_mod(z, o, gt1, ln1_g[l], ln1_b[l], sh2, sc2, n_ctx, alpha, transpose=True)
        f = _peer(ht, peer_wq[l], peer_subkeys[l], peer_u, peer_v, l).reshape(z.shape)
        nxt = mods[min(l + 1, depth - 1)]
        z, h = norm_mod(z, f, gt2, ln2_g[l], ln2_b[l], nxt[0], nxt[1], n_ctx, alpha, transpose=False)
    return z


def kernel(x, c, ctx, c_ctx, ada_w, ada_b, w_in, lru_conv_w, lru_conv_b, lru_wa, lru_ba, lru_wx, lru_bx, lru_lambda, rwkv_mu, rwkv_w0, rwkv_w2, rwkv_a0, rwkv_a2, rwkv_g2, rwkv_k_k, rwkv_k_a, rwkv_r_k, rwkv_gn_g, rwkv_gn_b, w_out, ln1_g, ln1_b, peer_wq, peer_subkeys, peer_u, peer_v, ln2_g, ln2_b):
    return _forward(x, c, ctx, c_ctx, ada_w, ada_b, w_in, lru_conv_w, lru_conv_b, lru_wa, lru_ba,
                    lru_wx, lru_bx, lru_lambda, rwkv_mu, rwkv_w0, rwkv_w2, rwkv_a0, rwkv_a2, rwkv_g2,
                    rwkv_k_k, rwkv_k_a, rwkv_r_k, rwkv_gn_g, rwkv_gn_b, w_out, ln1_g, ln1_b,
                    peer_wq, peer_subkeys, peer_u, peer_v, ln2_g, ln2_b)
```

```python
import functools
import math

import jax
import jax.numpy as jnp
from jax import lax
from jax.experimental import pallas as pl
from jax.experimental.pallas import tpu as pltpu

F32 = jnp.float32
BF16 = jnp.bfloat16
HI = lax.Precision.HIGHEST

GRID_W = 64
N_DIR = 2
LRU_WIDTH = 1024
LRU_C = 8.0
CONV_WIDTH = 4
RWKV_WIDTH = 1024
HEAD_DIM = 64
DECAY_LORA = 64
AAA_LORA = 64
GATE_LORA = 160
PEER_HEADS = 8
PEER_N_KEYS = 128
PEER_TOPK = 16
LN_EPS = 1e-5
GN_EPS = 64e-5

LANES = 128
SUBLANES = 8
PAIR = LANES
CHUNK = 64
VMEM_LIMIT = 56 * 1024 * 1024


def _pick_block(n, target, align):
    best = None
    for b in range(align, min(n, target) + 1, align):
        if n % b == 0:
            best = b
    return best if best is not None else n


def _dot_nt(a, b):
    return lax.dot_general(a, b, (((1,), (1,)), ((), ())), preferred_element_type=F32)


def _dot_tn(a, b):
    return lax.dot_general(a, b, (((0,), (0,)), ((), ())), preferred_element_type=F32)


def _dot(a, b):
    return jnp.dot(a, b, preferred_element_type=F32)


def _dot_hi(a, b):
    return jnp.dot(a, b, precision=HI, preferred_element_type=F32)


def _split(x, n):
    out = []
    for _ in range(n - 1):
        p = x.astype(BF16)
        out.append(p)
        x = x - p.astype(F32)
    out.append(x.astype(BF16))
    return out


def _dot_exact_rhs(a, b, n):
    b = b.astype(BF16)
    acc = None
    for p in _split(a, n):
        d = _dot(p, b)
        acc = d if acc is None else acc + d
    return acc


def _dot_exact_lhs(a, b, n):
    a = a.astype(BF16)
    acc = None
    for p in _split(b, n):
        d = _dot(a, p)
        acc = d if acc is None else acc + d
    return acc


def _dot_3pass(a, b):
    ah, al = _split(a, 2)
    bh, bl = _split(b, 2)
    return _dot(ah, bh) + (_dot(ah, bl) + _dot(al, bh))


def _softplus(x):
    return jnp.maximum(x, 0.0) + jnp.log1p(jnp.exp(-jnp.abs(x)))


def _mm_body(x_ref, w_ref, o_ref):
    o_ref[...] = _dot(x_ref[...].astype(BF16), w_ref[...].astype(BF16)).astype(o_ref.dtype)


def matmul(x, w, col0=0, n_cols=None, layer=None, bm_target=1088, bn_target=512):
    M, K = x.shape
    N = w.shape[-1] - col0 if n_cols is None else n_cols
    bm = _pick_block(M, bm_target, 8)
    bn = _pick_block(math.gcd(N, col0) if col0 else N, bn_target, LANES)
    j0 = col0 // bn
    if layer is None:
        w_spec = pl.BlockSpec((K, bn), lambda i, j: (0, j0 + j))
    else:
        w_spec = pl.BlockSpec((None, K, bn), lambda i, j: (layer, 0, j0 + j))
    return pl.pallas_call(
        _mm_body,
        grid=(M // bm, N // bn),
        in_specs=[pl.BlockSpec((bm, K), lambda i, j: (i, 0)), w_spec],
        out_specs=pl.BlockSpec((bm, bn), lambda i, j: (i, j)),
        out_shape=jax.ShapeDtypeStruct((M, N), F32),
        compiler_params=pltpu.CompilerParams(
            dimension_semantics=("parallel", "parallel"), vmem_limit_bytes=VMEM_LIMIT),
        name="matmul",
    )(x, w)


def _prep_body(p_lru_ref, lo_lru_ref, hi_lru_ref, p_rw_ref, lo_rw_ref, hi_rw_ref,
               cw_ref, cb_ref, mu_ref, xc_ref, rw_ref, *, nb_ctx):
    t = pl.program_id(1)
    tb = p_lru_ref.shape[1]
    first = (t == 0) | (t == nb_ctx)
    last = (t == nb_ctx - 1) | (t == pl.num_programs(1) - 1)
    row = lax.broadcasted_iota(jnp.int32, (tb, 1), 0)

    def shifted(x, lo, hi, d):
        if d == 0:
            return x
        y = pltpu.roll(x, (-d) % tb, axis=0)
        if d < 0:
            for i in range(-d):
                fill = jnp.where(first, 0.0, lo[SUBLANES + d + i:SUBLANES + d + i + 1, :])
                y = jnp.where(row == i, fill, y)
        else:
            for i in range(d):
                fill = jnp.where(last, 0.0, hi[i:i + 1, :])
                y = jnp.where(row == tb - d + i, fill, y)
        return y

    x = p_lru_ref[0]
    lo, hi = lo_lru_ref[0], hi_lru_ref[0]
    acc = cb_ref[...] + jnp.zeros_like(x)
    for tap in range(CONV_WIDTH):
        acc = acc + shifted(x, lo, hi, tap - CONV_WIDTH // 2) * cw_ref[tap:tap + 1, :]
    xc_ref[0] = acc

    z = p_rw_ref[0]
    lo, hi = lo_rw_ref[0], hi_rw_ref[0]
    rw_ref[0] = z + mu_ref[0:1, :] * (shifted(z, lo, hi, -1) - z) + mu_ref[1:2, :] * (shifted(z, lo, hi, 1) - z)


def mixer_prep(p_lg, p_rw, conv_w, conv_b, mu_p, n_ctx):
    B, TT, RW = p_rw.shape
    C = LRU_WIDTH
    tb = _pick_block(math.gcd(n_ctx, TT - n_ctx), 256, SUBLANES)
    nb, nb_ctx, r8 = TT // tb, n_ctx // tb, tb // SUBLANES
    cur = lambda b, t: (b, t, 0)
    lo = lambda b, t: (b, jnp.maximum(t * r8 - 1, 0), 0)
    hi = lambda b, t: (b, jnp.minimum((t + 1) * r8, TT // SUBLANES - 1), 0)
    return pl.pallas_call(
        functools.partial(_prep_body, nb_ctx=nb_ctx),
        grid=(B, nb),
        in_specs=[pl.BlockSpec((1, tb, C), cur), pl.BlockSpec((1, SUBLANES, C), lo),
                  pl.BlockSpec((1, SUBLANES, C), hi),
                  pl.BlockSpec((1, tb, RW), cur), pl.BlockSpec((1, SUBLANES, RW), lo),
                  pl.BlockSpec((1, SUBLANES, RW), hi),
                  pl.BlockSpec(conv_w.shape, lambda b, t: (0, 0)),
                  pl.BlockSpec((1, C), lambda b, t: (0, 0)),
                  pl.BlockSpec(mu_p.shape, lambda b, t: (0, 0))],
        out_specs=[pl.BlockSpec((1, tb, C), cur), pl.BlockSpec((1, tb, RW), cur)],
        out_shape=[jax.ShapeDtypeStruct((B, TT, C), F32), jax.ShapeDtypeStruct((B, TT, RW), F32)],
        compiler_params=pltpu.CompilerParams(
            dimension_semantics=("parallel", "parallel"), vmem_limit_bytes=VMEM_LIMIT),
        name="mixer_prep",
    )(p_lg, p_lg, p_lg, p_rw, p_rw, p_rw, conv_w, conv_b[None], mu_p)


def _scan_block(g, t, n_batch, nb_ctx, nb):
    bwd = jnp.where(t < nb_ctx, nb_ctx - 1 - t, nb - 1 - (t - nb_ctx))
    return jnp.where(g // n_batch == 1, bwd, t)


def _lru_body(xc_ref, w_ref, ba_ref, bx_ref, lam_ref, y_ref, a_scr, b_scr, h_scr, *, n_batch):
    tb = xc_ref.shape[1]
    n_pair = xc_ref.shape[2] // PAIR
    rev = pl.program_id(0) // n_batch == 1

    @pl.when(pl.program_id(1) == 0)
    def _():
        h_scr[...] = jnp.zeros_like(h_scr)

    xc = xc_ref[0]
    ra, ia = [], []
    for p in range(n_pair):
        g = _dot(xc[:, p * PAIR:(p + 1) * PAIR].astype(BF16), w_ref[0, p])
        ra.append(g[:, :PAIR])
        ia.append(g[:, PAIR:])
    r = jax.nn.sigmoid(jnp.concatenate(ra, axis=1) + ba_ref[0])
    i = jax.nn.sigmoid(jnp.concatenate(ia, axis=1) + bx_ref[0])
    log_a = -LRU_C * r * _softplus(-lam_ref[0])
    a_scr[...] = jnp.exp(log_a)
    b_scr[...] = jnp.sqrt(jnp.maximum(1.0 - jnp.exp(2.0 * log_a), 0.0)) * (i * xc)

    def step(t, h):
        tt = jnp.where(rev, tb - 1 - t, t)
        h = a_scr[pl.ds(tt, 1), :] * h + b_scr[pl.ds(tt, 1), :]
        y_ref[0, pl.ds(tt, 1), :] = h
        return h

    h_scr[...] = lax.fori_loop(0, tb, step, h_scr[...], unroll=8)


def lru_scan(xc, w_bd, ba, bx, lam, n_ctx):
    B, TT, C = xc.shape
    tb = _pick_block(math.gcd(n_ctx, TT - n_ctx), 256, 8)
    nb, nb_ctx = TT // tb, n_ctx // tb
    seq = lambda g, t: (g % B, _scan_block(g, t, B, nb_ctx, nb), 0)
    dmap = lambda g, t: (g // B, 0, 0)
    return pl.pallas_call(
        functools.partial(_lru_body, n_batch=B),
        grid=(N_DIR * B, nb),
        in_specs=[pl.BlockSpec((1, tb, C), seq),
                  pl.BlockSpec((1, C // PAIR, PAIR, 2 * PAIR), lambda g, t: (g // B, 0, 0, 0)),
                  pl.BlockSpec((1, 1, C), dmap),
                  pl.BlockSpec((1, 1, C), dmap),
                  pl.BlockSpec((1, 1, C), dmap)],
        out_specs=pl.BlockSpec((1, tb, C), lambda g, t: (g, _scan_block(g, t, B, nb_ctx, nb), 0)),
        out_shape=jax.ShapeDtypeStruct((N_DIR * B, TT, C), F32),
        scratch_shapes=[pltpu.VMEM((tb, C), F32), pltpu.VMEM((tb, C), F32), pltpu.VMEM((1, C), F32)],
        compiler_params=pltpu.CompilerParams(
            dimension_semantics=("parallel", "arbitrary"), vmem_limit_bytes=VMEM_LIMIT),
        name="lru_scan",
    )(xc, w_bd, ba, bx, lam)


def _rwkv_body(r_ref, k_ref, v_ref, wd_ref, ad_ref, w0_ref, w2_ref, a0_ref, a2_ref,
               kk_ref, ka_ref, rk_ref, y_ref, bo_ref, s_ref, *, n_batch):
    L = r_ref.shape[1]
    C = r_ref.shape[2]
    n_pair = C // PAIR
    pairs = range(n_pair)
    sgn = jnp.where(pl.program_id(0) // n_batch == 1, -1, 1)

    @pl.when(pl.program_id(1) == 0)
    def _():
        s_ref[...] = jnp.zeros_like(s_ref)

    wpre = w0_ref[0] + _dot_3pass(jnp.tanh(wd_ref[0]), w2_ref[0])
    ld = -jnp.exp(-_softplus(-wpre) - 0.5)
    alr = jax.nn.sigmoid(a0_ref[0] + _dot_3pass(ad_ref[0], a2_ref[0]))

    ti = lax.broadcasted_iota(jnp.int32, (L, L), 0)
    tj = lax.broadcasted_iota(jnp.int32, (L, L), 1)
    cs = _dot_exact_lhs((((ti - tj) * sgn) >= 0).astype(F32), ld, 3)
    cs_end = jnp.where(sgn < 0, cs[0:1, :], cs[L - 1:L, :])

    ri = lax.broadcasted_iota(jnp.int32, (PAIR, PAIR), 0)
    ci = lax.broadcasted_iota(jnp.int32, (PAIR, PAIR), 1)
    same = (ri // HEAD_DIM) == (ci // HEAD_DIM)
    bd_ones = same.astype(F32)
    order = (ri - ci) * sgn
    strict = same & (order > 0)
    incl = same & (order >= 0)
    eye = (ri == ci).astype(F32)
    m0 = lax.broadcasted_iota(jnp.int32, (L, PAIR), 1) < HEAD_DIM

    def dup(z):
        return jnp.concatenate([z, z], axis=0)

    def slab(z):
        return jnp.where(m0, z[:L], z[L:])

    at, rt, v2, bh, kh, vv, lhs4, bt2, kt2, g_end = [], [], [], [], [], [], [], [], [], []
    for p in pairs:
        sl = slice(p * PAIR, (p + 1) * PAIR)
        r = r_ref[0, :, sl]
        k = k_ref[0, :, sl]
        v = v_ref[0, :, sl]
        ld_p, alr_p, cs_p, cs_l = ld[:, sl], alr[:, sl], cs[:, sl], cs_end[:, sl]
        kk0 = k * kk_ref[:, sl]
        ssq = _dot_exact_rhs(kk0 * kk0, bd_ones, 2)
        kk = kk0 * lax.rsqrt(jnp.maximum(ssq, 1e-24))
        kd = k * (1.0 + (alr_p - 1.0) * ka_ref[:, sl])
        b = kk * alr_p
        bo_ref[0, :, sl] = _dot_exact_rhs(r * kd * rk_ref[:, sl], bd_ones, 2) * v
        g_inv = jnp.exp(-cs_p)
        g_rel = jnp.exp(cs_l - cs_p)
        at_p = -kk * jnp.exp(cs_p - ld_p)
        rt_p = r * jnp.exp(cs_p)
        zero = jnp.zeros_like(at_p)
        lhs4.append(jnp.concatenate([jnp.where(m0, at_p, zero), jnp.where(m0, zero, at_p),
                                     jnp.where(m0, rt_p, zero), jnp.where(m0, zero, rt_p)], axis=0).astype(BF16))
        bt2.append(dup(b * g_inv).astype(BF16))
        kt2.append(dup(kd * g_inv).astype(BF16))
        at.append(at_p)
        rt.append(rt_p)
        vv.append(v)
        v2.append(dup(v).astype(BF16))
        bh.append((b * g_rel).astype(BF16))
        kh.append((kd * g_rel).astype(BF16))
        g_end.append(jnp.exp(cs_l))

    ab = [_dot_nt(lhs4[p], bt2[p]) for p in pairs]
    ak = [_dot_nt(lhs4[p], kt2[p]) for p in pairs]
    a_ab = [jnp.where(strict, ab[p][:2 * L], 0.0) for p in pairs]
    a_rb = [jnp.where(incl, ab[p][2 * L:], 0.0).astype(BF16) for p in pairs]
    a_ak = [jnp.where(strict, ak[p][:2 * L], 0.0).astype(BF16) for p in pairs]
    a_rk = [jnp.where(incl, ak[p][2 * L:], 0.0).astype(BF16) for p in pairs]
    av = [_dot(a_ak[p], v2[p]) for p in pairs]
    kv = [_dot(a_rk[p], v2[p]) for p in pairs]

    def off_diag(blk):
        rb = ((ri % (2 * blk)) >= blk).astype(jnp.int32)
        cb = ((ci % (2 * blk)) >= blk).astype(jnp.int32)
        return same & ((ri // (2 * blk)) == (ci // (2 * blk))) & (((rb - cb) * sgn) == 1)

    m1 = off_diag(1)
    tinv = [eye + jnp.where(m1, a_ab[p], 0.0) for p in pairs]
    blk = 2
    while blk < L:
        mb = off_diag(blk)
        tb = [tinv[p].astype(BF16) for p in pairs]
        t1 = [_dot(tb[p], jnp.where(mb, a_ab[p], 0.0).astype(BF16)).astype(BF16) for p in pairs]
        tinv = [tinv[p] + _dot(t1[p], tb[p]) for p in pairs]
        blk *= 2

    x = [_dot(tinv[p].astype(BF16), jnp.concatenate([dup(at[p]), av[p]], axis=1).astype(BF16)) for p in pairs]
    abar = [slab(x[p][:, :PAIR]) for p in pairs]
    uv = [slab(x[p][:, PAIR:]) for p in pairs]
    z = [_dot(a_rb[p], jnp.concatenate([dup(abar[p]), dup(uv[p])], axis=1).astype(BF16)) for p in pairs]
    rbar = [(rt[p] + slab(z[p][:, :PAIR])).astype(BF16) for p in pairs]
    yv = [slab(z[p][:, PAIR:] + kv[p]) for p in pairs]
    mp = [jnp.where(same, _dot_tn(abar[p].astype(BF16), bh[p]), 0.0).astype(BF16) for p in pairs]
    sv = [jnp.where(same, _dot_tn(jnp.concatenate([uv[p], vv[p]], axis=0).astype(BF16),
                                  jnp.concatenate([bh[p], kh[p]], axis=0)), 0.0) for p in pairs]
    for p in pairs:
        sl = slice(p * PAIR, (p + 1) * PAIR)
        s0 = s_ref[p]
        s0b = s0.astype(BF16)
        y_ref[0, :, sl] = _dot_nt(rbar[p], s0b) + yv[p]
        s_ref[p] = s0 * g_end[p] + _dot(s0b, mp[p]) + sv[p]


def rwkv_scan(rw, w0, w2p, a0, a2p, k_k, k_a, r_k, n_ctx):
    B, TT, _ = rw.shape
    C = RWKV_WIDTH
    L = CHUNK
    assert n_ctx % L == 0 and TT % L == 0 and 2 * L == PAIR and C % PAIR == 0
    nb, nb_ctx = TT // L, n_ctx // L
    dmap3 = lambda g, c: (g // B, 0, 0)
    cmap = lambda g, c: (0, 0)
    seq = lambda j: pl.BlockSpec((1, L, C), lambda g, c: (g % B, _scan_block(g, c, B, nb_ctx, nb), j))
    lora = lambda j: pl.BlockSpec((1, L, PAIR), lambda g, c: (g % B, _scan_block(g, c, B, nb_ctx, nb), j))
    out = pl.BlockSpec((1, L, C), lambda g, c: (g, _scan_block(g, c, B, nb_ctx, nb), 0))
    return pl.pallas_call(
        functools.partial(_rwkv_body, n_batch=B),
        grid=(N_DIR * B, nb),
        in_specs=[seq(0), seq(1), seq(2), lora(3 * C // PAIR), lora(3 * C // PAIR + 1),
                  pl.BlockSpec((1, 1, C), dmap3), pl.BlockSpec((1, PAIR, C), dmap3),
                  pl.BlockSpec((1, 1, C), dmap3), pl.BlockSpec((1, PAIR, C), dmap3),
                  pl.BlockSpec((1, C), cmap), pl.BlockSpec((1, C), cmap), pl.BlockSpec((1, C), cmap)],
        out_specs=[out, out],
        out_shape=[jax.ShapeDtypeStruct((N_DIR * B, TT, C), F32)] * 2,
        scratch_shapes=[pltpu.VMEM((C // PAIR, PAIR, PAIR), F32)],
        compiler_params=pltpu.CompilerParams(
            dimension_semantics=("parallel", "arbitrary"), vmem_limit_bytes=VMEM_LIMIT),
        name="rwkv_scan",
    )(rw, rw, rw, rw, rw, w0, w2p, a0, a2p, k_k, k_a, r_k)


def _merge_body(pg_ref, yf_ref, yb_ref, rf_ref, rb_ref, bf_ref, bb_ref, gd_ref,
                gng_ref, gnb_ref, g2_ref, wo_ref, o_ref):
    left = jax.nn.gelu(pg_ref[0]) * (yf_ref[0] + yb_ref[0])
    ys = rf_ref[0] + rb_ref[0]
    ri = lax.broadcasted_iota(jnp.int32, (PAIR, PAIR), 0)
    ci = lax.broadcasted_iota(jnp.int32, (PAIR, PAIR), 1)
    bd_ones = ((ri // HEAD_DIM) == (ci // HEAD_DIM)).astype(F32)
    yn = []
    for p in range(ys.shape[1] // PAIR):
        sl = slice(p * PAIR, (p + 1) * PAIR)
        yp = ys[:, sl]
        d = yp - _dot_exact_rhs(yp, bd_ones, 3) * (1.0 / HEAD_DIM)
        var = _dot_exact_rhs(d * d, bd_ones, 3) * (1.0 / HEAD_DIM)
        yn.append(d * lax.rsqrt(var + GN_EPS))
    yn = jnp.concatenate(yn, axis=1) * gng_ref[...] + gnb_ref[...]
    gate = _dot(jax.nn.sigmoid(gd_ref[0]).astype(BF16), g2_ref[...])
    right = (yn + (bf_ref[0] + bb_ref[0])) * gate
    merged = jnp.concatenate([left, right], axis=1).astype(BF16)
    o_ref[0] = _dot(merged, wo_ref[...])


def mixer_out(p_lg, y, yr, bo, rw, gn_g, gn_b, g2p, w_out):
    B, TT, _ = p_lg.shape
    C = LRU_WIDTH
    GW = g2p.shape[0]
    D = w_out.shape[1]
    bm = _pick_block(TT, 256, SUBLANES)
    fwd = pl.BlockSpec((1, bm, C), lambda b, t: (b, t, 0))
    bwd = pl.BlockSpec((1, bm, C), lambda b, t: (B + b, t, 0))
    const = lambda shape: pl.BlockSpec(shape, lambda b, t: (0, 0))
    return pl.pallas_call(
        _merge_body,
        grid=(B, TT // bm),
        in_specs=[pl.BlockSpec((1, bm, C), lambda b, t: (b, t, 1)),
                  fwd, bwd, fwd, bwd, fwd, bwd,
                  pl.BlockSpec((1, bm, GW), lambda b, t: (b, t, rw.shape[2] // GW - 1)),
                  const((1, C)), const((1, C)), const(g2p.shape), const(w_out.shape)],
        out_specs=pl.BlockSpec((1, bm, D), lambda b, t: (b, t, 0)),
        out_shape=jax.ShapeDtypeStruct((B, TT, D), F32),
        compiler_params=pltpu.CompilerParams(
            dimension_semantics=("parallel", "parallel"), vmem_limit_bytes=VMEM_LIMIT),
        name="mixer_out",
    )(p_lg, y, y, yr, yr, bo, bo, rw, gn_g[None], gn_b[None], g2p, w_out)


def _norm_mod_body(z_ref, o_ref, gate_ref, g_ref, b_ref, sh_ref, sc_ref, zo_ref, h_ref, *, alpha, transpose):
    z = alpha * z_ref[0] + gate_ref[0, 0] * o_ref[0]
    mu = jnp.mean(z, axis=-1, keepdims=True)
    d = z - mu
    var = jnp.mean(d * d, axis=-1, keepdims=True)
    zn = d * lax.rsqrt(var + LN_EPS) * g_ref[...] + b_ref[...]
    zo_ref[0] = zn
    h = zn * (1.0 + sc_ref[0, 0]) + sh_ref[0, 0]
    if transpose:
        h_ref[...] = h.T.astype(BF16)
    else:
        h_ref[0] = h.astype(BF16)


def norm_mod(z, o, gate, ln_g, ln_b, shift, scale, n_ctx, alpha, transpose):
    B, TT, D = z.shape
    tb = _pick_block(math.gcd(n_ctx, TT - n_ctx) if n_ctx else TT, 256, LANES)
    nb, nb_ctx = TT // tb, n_ctx // tb
    cur = pl.BlockSpec((1, tb, D), lambda b, t: (b, t, 0))
    seg = pl.BlockSpec((1, 1, 1, D), lambda b, t: (b, jnp.where(t >= nb_ctx, 1, 0), 0, 0))
    vec = pl.BlockSpec((1, D), lambda b, t: (0, 0))
    if transpose:
        h_spec = pl.BlockSpec((D, tb), lambda b, t: (0, b * nb + t))
        h_shape = jax.ShapeDtypeStruct((D, B * TT), BF16)
    else:
        h_spec, h_shape = cur, jax.ShapeDtypeStruct((B, TT, D), BF16)
    return pl.pallas_call(
        functools.partial(_norm_mod_body, alpha=alpha, transpose=transpose),
        grid=(B, nb),
        in_specs=[cur, cur, seg, vec, vec, seg, seg],
        out_specs=[cur, h_spec],
        out_shape=[jax.ShapeDtypeStruct((B, TT, D), F32), h_shape],
        compiler_params=pltpu.CompilerParams(
            dimension_semantics=("parallel", "parallel"), vmem_limit_bytes=VMEM_LIMIT),
        name="norm_mod",
    )(z, o, gate, ln_g[None], ln_b[None], shift, scale)


def _top_values(x, n):
    rows = lax.broadcasted_iota(jnp.int32, x.shape, 0).astype(F32)
    out = []
    for _ in range(n):
        m = jnp.max(x, axis=0, keepdims=True)
        out.append(m)
        first = jnp.min(jnp.where(x == m, rows, float(x.shape[0])), axis=0, keepdims=True)
        x = jnp.where(rows == first, -jnp.inf, x)
    return out


def _candidate_rows(t1, t2, n):
    k = len(t1)
    t1c = jnp.concatenate(t1, axis=0)
    t2c = jnp.concatenate(t2, axis=0)
    row = lax.broadcasted_iota(jnp.int32, (SUBLANES, t1c.shape[1]), 0)
    groups = []
    for j in range(k):
        cnt = min(k, n // (j + 1))
        if cnt <= 1:
            break
        for g in range(0, cnt, SUBLANES):
            piece = t1c[g:g + SUBLANES] + t2[j]
            groups.append(piece if cnt - g >= SUBLANES else jnp.where(row < cnt - g, piece, -jnp.inf))
    j0 = j
    for g in range(j0, k, SUBLANES):
        piece = t1[0] + t2c[g:g + SUBLANES]
        groups.append(piece if k - g >= SUBLANES else jnp.where(row < k - g, piece, -jnp.inf))
    return jnp.concatenate(groups, axis=0)


def _top_values_distinct(x, n):
    start = jnp.sum((x == -jnp.inf).astype(F32), axis=0, keepdims=True)
    out = []
    for _ in range(n):
        m = jnp.max(x, axis=0, keepdims=True)
        out.append(m)
        x = jnp.where(x == m, -jnp.inf, x)
    return out, jnp.sum((x == -jnp.inf).astype(F32), axis=0, keepdims=True) - start


def _peer_score_body(ht_ref, wqt_ref, sk_ref, t1_ref, s2_ref, e1_ref, e2_ref):
    n_heads = sk_ref.shape[0]
    dq = sk_ref.shape[3]
    qt = _dot(wqt_ref[...], ht_ref[...])
    scores = [[_dot_3pass(sk_ref[h, half], qt[(2 * h + half) * dq:(2 * h + half + 1) * dq, :])
               for half in range(2)] for h in range(n_heads)]

    def emit(h, s, t1, t2, top):
        zsum = jnp.exp(top[0] - top[0])
        for t in top[1:PEER_TOPK]:
            zsum = zsum + jnp.exp(t - top[0])
        tau = 0.5 * (top[PEER_TOPK - 1] + top[PEER_TOPK])
        t1_ref[h] = jnp.where(s[0] >= t1[-1], tau - s[0], jnp.inf)
        s2_ref[h] = jnp.where(s[1] >= t2[-1], s[1], -jnp.inf)
        e1_ref[h] = jnp.exp(s[0] - t1[0]) / zsum
        e2_ref[h] = jnp.exp(s[1] - t2[0])

    excess = jnp.zeros((1, qt.shape[1]), F32)
    for h in range(n_heads):
        s = scores[h]
        t1, n1 = _top_values_distinct(s[0], PEER_TOPK)
        t2, n2 = _top_values_distinct(s[1], PEER_TOPK)
        top, n3 = _top_values_distinct(_candidate_rows(t1, t2, PEER_TOPK + 1), PEER_TOPK + 1)
        excess = excess + (n1 - PEER_TOPK) + (n2 - PEER_TOPK) + (n3 - (PEER_TOPK + 1))
        emit(h, s, t1, t2, top)

    @pl.when(jnp.max(excess) > 0.0)
    def _():
        for h in range(n_heads):
            s = scores[h]
            t1 = _top_values(s[0], PEER_TOPK)
            t2 = _top_values(s[1], PEER_TOPK)
            emit(h, s, t1, t2, _top_values(_candidate_rows(t1, t2, PEER_TOPK + 1), PEER_TOPK + 1))


def peer_scores(ht, wqt, subkeys):
    D, M = ht.shape
    nh, _, nk, dq = subkeys.shape
    tb = _pick_block(M, 256, LANES)
    big = pl.BlockSpec((nh, nk, tb), lambda i: (0, 0, i))
    shp = jax.ShapeDtypeStruct((nh, nk, M), F32)
    return pl.pallas_call(
        _peer_score_body,
        grid=(M // tb,),
        in_specs=[pl.BlockSpec((D, tb), lambda i: (0, i)),
                  pl.BlockSpec(wqt.shape, lambda i: (0, 0)),
                  pl.BlockSpec(subkeys.shape, lambda i: (0, 0, 0, 0))],
        out_specs=[big, big, big, big],
        out_shape=[shp, shp, shp, shp],
        compiler_params=pltpu.CompilerParams(
            dimension_semantics=("parallel",), vmem_limit_bytes=VMEM_LIMIT),
        name="peer_scores",
    )(ht, wqt, subkeys)


def _tables_body(u_ref, v_ref, ub_ref, vt_ref):
    ub_ref[...] = u_ref[...].astype(BF16)
    vt_ref[...] = v_ref[...].T.astype(BF16)


def peer_tables(u, v, layer):
    _, E, D = u.shape
    eb = _pick_block(E, 512, LANES)
    src = pl.BlockSpec((None, eb, D), lambda i: (layer, i, 0))
    return pl.pallas_call(
        _tables_body,
        grid=(E // eb,),
        in_specs=[src, src],
        out_specs=[pl.BlockSpec((eb, D), lambda i: (i, 0)), pl.BlockSpec((D, eb), lambda i: (0, i))],
        out_shape=[jax.ShapeDtypeStruct((E, D), BF16), jax.ShapeDtypeStruct((D, E), BF16)],
        compiler_params=pltpu.CompilerParams(
            dimension_semantics=("parallel",), vmem_limit_bytes=VMEM_LIMIT),
        name="peer_tables",
    )(u, v)


E2_TILE = 64


def _peer_expert_body(ht_ref, u_ref, vt_ref, t1_ref, s2_ref, e1_ref, e2_ref, o_ref, act_scr, x_scr, ot_scr):
    j = pl.program_id(1)
    n_heads, nk, tb = s2_ref.shape
    eb = u_ref.shape[0]
    per = eb // nk

    @pl.when(j == 0)
    def _():
        ot_scr[...] = jnp.zeros_like(ot_scr)

    act_scr[...] = _dot(u_ref[...], ht_ref[...])
    t1full = [[t1_ref[h, pl.ds(j * per + c, 1), :] for h in range(n_heads)] for c in range(per)]
    e1full = [[e1_ref[h, pl.ds(j * per + c, 1), :] for h in range(n_heads)] for c in range(per)]
    for tj in range(tb // LANES):
        tsl = slice(tj * LANES, (tj + 1) * LANES)
        t1rows = [[r[:, tsl] for r in rows] for rows in t1full]
        e1rows = [[r[:, tsl] for r in rows] for rows in e1full]
        for et in range(nk // E2_TILE):
            esl = slice(et * E2_TILE, (et + 1) * E2_TILE)
            accs = [jnp.zeros((E2_TILE, LANES), F32) for _ in range(per)]
            for h in range(n_heads):
                s2t = s2_ref[h, esl, tsl]
                e2t = e2_ref[h, esl, tsl]
                for c in range(per):
                    accs[c] = accs[c] + jnp.where(s2t >= t1rows[c][h], e2t, 0.0) * e1rows[c][h]
            for c in range(per):
                rsl = slice(c * nk + et * E2_TILE, c * nk + (et + 1) * E2_TILE)
                a = act_scr[rsl, tsl]
                gelu = 0.5 * a * (1.0 + lax.erf(a * (2.0 ** -0.5)))
                x_scr[rsl, tsl] = (accs[c] * gelu).astype(BF16)
    ot_scr[...] += _dot(vt_ref[...], x_scr[...])

    @pl.when(j == pl.num_programs(1) - 1)
    def _():
        o_ref[...] = ot_scr[...].T


def peer_experts(ht, u, vt, t1, s2, e1, e2):
    D, M = ht.shape
    nh, nk, _ = s2.shape
    E = u.shape[0]
    tb = _pick_block(M, 512, LANES)
    eb = _pick_block(E, 1024, nk)
    big = pl.BlockSpec((nh, nk, tb), lambda i, j: (0, 0, i))
    return pl.pallas_call(
        _peer_expert_body,
        grid=(M // tb, E // eb),
        in_specs=[pl.BlockSpec((D, tb), lambda i, j: (0, i)),
                  pl.BlockSpec((eb, D), lambda i, j: (j, 0)),
                  pl.BlockSpec((D, eb), lambda i, j: (0, j)),
                  big, big, big, big],
        out_specs=pl.BlockSpec((tb, D), lambda i, j: (i, 0)),
        out_shape=jax.ShapeDtypeStruct((M, D), F32),
        scratch_shapes=[pltpu.VMEM((eb, tb), F32), pltpu.VMEM((eb, tb), BF16), pltpu.VMEM((D, tb), F32)],
        compiler_params=pltpu.CompilerParams(
            dimension_semantics=("parallel", "arbitrary"), vmem_limit_bytes=VMEM_LIMIT),
        name="peer_experts",
    )(ht, u, vt, t1, s2, e1, e2)


def _modulate(z, shift, scale):
    return z * (1.0 + scale) + shift


def _grid_pos_embed(rows, dim):
    t = jnp.arange(rows * GRID_W)
    row = (t // GRID_W).astype(F32)
    col = (t % GRID_W).astype(F32)
    quarter = dim // 4
    freq = 1.0 / (10000.0 ** (jnp.arange(quarter, dtype=F32) / quarter))

    def sincos(p):
        ang = p[:, None] * freq[None, :]
        return jnp.concatenate([jnp.sin(ang), jnp.cos(ang)], axis=-1)

    return jnp.concatenate([sincos(row), sincos(col)], axis=-1)


def _pair_block_diag(w):
    nh = w.shape[0]
    w = w.reshape(nh // 2, 2, HEAD_DIM, HEAD_DIM)
    z = jnp.zeros_like(w[:, 0])
    top = jnp.concatenate([w[:, 0], z], axis=2)
    bot = jnp.concatenate([z, w[:, 1]], axis=2)
    return jnp.concatenate([top, bot], axis=1)


def _token_mixer(h_all, n_ctx, keep_ctx, w_in, conv_w, conv_b, wa, ba, wx, bx, lam,
                 mu, w0, w2, a0, a2, g2, k_k, k_a, r_k, gn_g, gn_b, w_out):
    B, TT, D = h_all.shape
    C = LRU_WIDTH
    in_cols = w_in.shape[1]
    pad_cols = -in_cols % 512
    w_in_p = jnp.pad(w_in, ((0, 0), (0, pad_cols))).astype(BF16)
    hb = h_all.astype(BF16).reshape(B * TT, D)
    p_lg = matmul(hb, w_in_p, 0, 2 * C).reshape(B, TT, -1)
    p_rw = matmul(hb, w_in_p, 2 * C).reshape(B, TT, -1)
    mu_p = jnp.pad(mu, ((0, 0), (0, p_rw.shape[-1] - mu.shape[1])))
    xc, rw = mixer_prep(p_lg, p_rw, conv_w, conv_b, mu_p, n_ctx)

    w_bd = jnp.concatenate([_pair_block_diag(wa.reshape((-1,) + wa.shape[2:])),
                            _pair_block_diag(wx.reshape((-1,) + wx.shape[2:]))], axis=2)
    w_bd = w_bd.reshape(N_DIR, C // PAIR, PAIR, 2 * PAIR).astype(BF16)
    y = lru_scan(xc, w_bd, ba[:, None, :], bx[:, None, :], lam[:, None, :], n_ctx)

    zl = jnp.zeros_like(w2[0])
    w2p = jnp.stack([jnp.concatenate([w2[0], zl], axis=0), jnp.concatenate([zl, w2[1]], axis=0)])
    a2p = jnp.stack([jnp.concatenate([a2[0], zl], axis=0), jnp.concatenate([zl, a2[1]], axis=0)])
    yr, bo = rwkv_scan(rw, w0[:, None, :], w2p, a0[:, None, :], a2p,
                       k_k[None], k_a[None], r_k.reshape(1, -1), n_ctx)

    gd_cols = rw.shape[-1] - 3 * RWKV_WIDTH - N_DIR * (DECAY_LORA + AAA_LORA)
    g2p = jnp.pad(g2, ((0, gd_cols - g2.shape[0]), (0, 0))).astype(BF16)
    out = mixer_out(p_lg, y, yr, bo, rw, gn_g, gn_b, g2p, w_out.astype(BF16))
    return out if keep_ctx else out[:, n_ctx:]


def _peer(ht, wq, subkeys, u, v, layer):
    t1, s2, e1, e2 = peer_scores(ht, wq.T.astype(BF16), subkeys)
    ub, vt = peer_tables(u, v, layer)
    return peer_experts(ht, ub, vt, t1, s2, e1, e2)


def _seg_mods(mod, B, D):
    lat = mod[:B].reshape(B, 6, D)
    ctx = jnp.broadcast_to(mod[B].reshape(1, 6, D), (B, 6, D))
    both = jnp.stack([ctx, lat], axis=1)
    return [both[:, :, i, None, :] for i in range(6)]


def _forward(x, c, ctx, c_ctx, ada_w, ada_b, w_in, lru_conv_w, lru_conv_b, lru_wa, lru_ba,
             lru_wx, lru_bx, lru_lambda, rwkv_mu, rwkv_w0, rwkv_w2, rwkv_a0, rwkv_a2, rwkv_g2,
             rwkv_k_k, rwkv_k_a, rwkv_r_k, rwkv_gn_g, rwkv_gn_b, w_out, ln1_g, ln1_b,
             peer_wq, peer_subkeys, peer_u, peer_v, ln2_g, ln2_b):
    B, T, D = x.shape
    depth = ada_w.shape[0]
    alpha = (2.0 * depth) ** 0.25
    n_ctx = ctx.shape[1]
    x = x + _grid_pos_embed(T // GRID_W, D).astype(x.dtype)[None]
    cin = jnp.concatenate([jax.nn.silu(c), jax.nn.silu(c_ctx)[None],
                           jnp.zeros((8 - B - 1, D), F32)], axis=0)
    mods = [_seg_mods(matmul(cin, ada_w, layer=l) + ada_b[l], B, D) for l in range(depth)]
    z = jnp.concatenate([ctx, x], axis=1)
    sh1, sc1 = mods[0][0], mods[0][1]
    h = jnp.concatenate([_modulate(ctx, sh1[:, 0], sc1[:, 0]), _modulate(x, sh1[:, 1], sc1[:, 1])],
                        axis=1).astype(BF16)
    for l in range(depth):
        keep_ctx = l < depth - 1
        _, _, gt1, sh2, sc2, gt2 = mods[l]
        o = _token_mixer(h, n_ctx, keep_ctx, w_in[l], lru_conv_w[l], lru_conv_b[l], lru_wa[l],
                         lru_ba[l], lru_wx[l], lru_bx[l], lru_lambda[l], rwkv_mu[l], rwkv_w0[l],
                         rwkv_w2[l], rwkv_a0[l], rwkv_a2[l], rwkv_g2[l], rwkv_k_k[l], rwkv_k_a[l],
                         rwkv_r_k[l], rwkv_gn_g[l], rwkv_gn_b[l], w_out[l])
        if not keep_ctx:
            z, n_ctx = z[:, n_ctx:], 0
        z, ht = norm_mod(z, o, gt1, ln1_g[l], ln1_b[l], sh2, sc2, n_ctx, alpha, transpose=True)
        f = _peer(ht, peer_wq[l], peer_subkeys[l], peer_u, peer_v, l).reshape(z.shape)
        nxt = mods[min(l + 1, depth - 1)]
        z, h = norm_mod(z, f, gt2, ln2_g[l], ln2_b[l], nxt[0], nxt[1], n_ctx, alpha, transpose=False)
    return z


def kernel(x, c, ctx, c_ctx, ada_w, ada_b, w_in, lru_conv_w, lru_conv_b, lru_wa, lru_ba, lru_wx, lru_bx, lru_lambda, rwkv_mu, rwkv_w0, rwkv_w2, rwkv_a0, rwkv_a2, rwkv_g2, rwkv_k_k, rwkv_k_a, rwkv_r_k, rwkv_gn_g, rwkv_gn_b, w_out, ln1_g, ln1_b, peer_wq, peer_subkeys, peer_u, peer_v, ln2_g, ln2_b):
    return _forward(x, c, ctx, c_ctx, ada_w, ada_b, w_in, lru_conv_w, lru_conv_b, lru_wa, lru_ba,
                    lru_wx, lru_bx, lru_lambda, rwkv_mu, rwkv_w0, rwkv_w2, rwkv_a0, rwkv_a2, rwkv_g2,
                    rwkv_k_k, rwkv_k_a, rwkv_r_k, rwkv_gn_g, rwkv_gn_b, w_out, ln1_g, ln1_b,
                    peer_wq, peer_subkeys, peer_u, peer_v, ln2_g, ln2_b)
```
